```python
import math
import jax
import jax.numpy as jnp
from jax import lax
import numpy as np

D_MODEL = 4096
BATCH = 4
SEQ = 2048
DEPTH = 4
DEC_BATCH = 8
DEC_SEQ = 1
PAST_LEN = 8192
PAGE_SIZE = 128

HD_V = 128
HD_C = 64
HD_QK = 2 * HD_C
H_A = (D_MODEL // 2) // HD_V
DK = 128
DV = 128
H_D = (D_MODEL - H_A * HD_V) // DV
ROT_DIM = HD_C // 4
ROPE_THETA = 500000.0
CONV = 4
CONV_CH = H_D * (2 * DK + DV)
GDN_CHUNK = 64
Q_BLOCK = 128
D_FF = -(-(8 * D_MODEL) // (3 * 256)) * 256
N_MOD = 6
EPS = 1e-6
SPLIT_SIZES = (H_A * HD_QK, H_A * HD_QK, H_A * HD_V, CONV_CH, H_D * DV, H_D, H_D)
N_IN = sum(SPLIT_SIZES)

kernel_name = "hybrid_diffattn_gated_deltanet_adaln_step"


def _split_proj(p):
    offs = [int(o) for o in np.cumsum(SPLIT_SIZES)[:-1]]
    return jnp.split(p, offs, axis=-1)


def _rmsnorm(x, g):
    xf = x.astype(jnp.float32)
    y = xf * lax.rsqrt(jnp.mean(xf * xf, axis=-1, keepdims=True) + EPS)
    return (y * g.astype(jnp.float32)).astype(x.dtype)


def _adaln(c, w, b):
    m = (c @ w + b).reshape(c.shape[0], N_MOD, 1, D_MODEL)
    return [m[:, i] for i in range(N_MOD)]


def _modulate(h, shift, scale):
    return h * (1 + scale) + shift


def _rope(x, pos):
    half = ROT_DIM // 2
    inv = ROPE_THETA ** (-jnp.arange(half, dtype=jnp.float32) * (2.0 / ROT_DIM))
    ang = pos.astype(jnp.float32)[:, None] * inv[None, :]
    cos = jnp.cos(ang)[None, :, None, None, :]
    sin = jnp.sin(ang)[None, :, None, None, :]
    x1 = x[..., :half].astype(jnp.float32)
    x2 = x[..., half:ROT_DIM].astype(jnp.float32)
    rot = jnp.concatenate([x1 * cos - x2 * sin, x2 * cos + x1 * sin], axis=-1).astype(x.dtype)
    return jnp.concatenate([rot, x[..., ROT_DIM:]], axis=-1)


def _lambda(lq1, lk1, lq2, lk2, lam_init):
    f = jnp.float32
    return (jnp.exp(jnp.sum(lq1.astype(f) * lk1.astype(f)))
            - jnp.exp(jnp.sum(lq2.astype(f) * lk2.astype(f))) + lam_init)


def _diff_weights(sc, mask, lam, dtype):
    p = jax.nn.softmax(jnp.where(mask, sc, -jnp.inf), axis=-1)
    return (p[:, 0] - lam * p[:, 1]).astype(dtype)


def _diff_attn_prompt(q, k, v, lam):
    B, T, H, _, HC = q.shape
    nb = T // Q_BLOCK
    qb = jnp.moveaxis(q.reshape(B, nb, Q_BLOCK, H, 2, HC), 1, 0)
    starts = jnp.arange(nb) * Q_BLOCK
    kpos = jnp.arange(T)
    scale = HC ** -0.5

    def one_block(args):
        qi, s = args
        sc = jnp.einsum('bqhcd,bkhcd->bchqk', qi, k).astype(jnp.float32) * scale
        qpos = s + jnp.arange(Q_BLOCK)
        mask = kpos[None, :] <= qpos[:, None]
        w = _diff_weights(sc, mask, lam, v.dtype)
        return jnp.einsum('bhqk,bkhd->bqhd', w, v)

    o = lax.map(one_block, (qb, starts))
    return jnp.moveaxis(o, 0, 1).reshape(B, T, H, v.shape[-1])


def _diff_attn_sample(q, k_new, v_new, k_pool, v_pool, page_table, lam):
    DB, DS, H, _, HC = q.shape
    k_past = k_pool[page_table].reshape(DB, -1, H, 2, HC)
    v_past = v_pool[page_table].reshape(DB, -1, H, v_new.shape[-1])
    P = k_past.shape[1]
    k_all = jnp.concatenate([k_past, k_new.astype(k_past.dtype)], axis=1)
    v_all = jnp.concatenate([v_past, v_new.astype(v_past.dtype)], axis=1)
    sc = jnp.einsum('bqhcd,bkhcd->bchqk', q, k_all.astype(q.dtype)).astype(jnp.float32) * (HC ** -0.5)
    kpos = jnp.arange(P + DS)
    qpos = P + jnp.arange(DS)
    mask = kpos[None, :] <= qpos[:, None]
    w = _diff_weights(sc, mask, lam, v_new.dtype)
    return jnp.einsum('bhqk,bkhd->bqhd', w, v_all.astype(v_new.dtype))


def _causal_conv(xpad, w, T):
    return sum(xpad[:, j:j + T] * w[j] for j in range(CONV))


def _l2n(x):
    return x * lax.rsqrt(jnp.sum(x * x, axis=-1, keepdims=True) + EPS)


def _gdn_prep(conv_out, b_raw, a_raw, a_log_l, dt_bias_l):
    f = jnp.float32
    B, T, _ = conv_out.shape
    act = jax.nn.silu(conv_out.astype(f))
    q, k, v = jnp.split(act, [H_D * DK, 2 * H_D * DK], axis=-1)
    q = _l2n(q.reshape(B, T, H_D, DK)) * (DK ** -0.5)
    k = _l2n(k.reshape(B, T, H_D, DK))
    v = v.reshape(B, T, H_D, DV)
    beta = jax.nn.sigmoid(b_raw.astype(f))
    g = -jnp.exp(a_log_l.astype(f)) * jax.nn.softplus(a_raw.astype(f) + dt_bias_l.astype(f))
    return q, k, v, g, beta


def _gdn_chunked(q, k, v, g, beta):
    B, T, H, _ = k.shape
    C = GDN_CHUNK
    N = T // C

    def chunks(a):
        return jnp.moveaxis(a.reshape((B, N, C, H) + a.shape[3:]), 3, 2)

    qc, kc, vc, bc = chunks(q), chunks(k), chunks(v), chunks(beta)
    gc = jnp.cumsum(chunks(g), axis=-1)
    idx = jnp.arange(C)
    causal = idx[:, None] >= idx[None, :]
    strict = idx[:, None] > idx[None, :]
    decay = jnp.exp(jnp.where(causal, gc[..., :, None] - gc[..., None, :], -jnp.inf))
    kb = kc * bc[..., None]
    lower = jnp.where(strict, jnp.einsum('bnhid,bnhjd->bnhij', kb, kc) * decay, 0.0)
    eye = jnp.eye(C, dtype=jnp.float32)
    tmat = lax.linalg.triangular_solve(eye + lower, jnp.broadcast_to(eye, lower.shape),
                                       left_side=True, lower=True)
    u = jnp.einsum('bnhij,bnhjd->bnhid', tmat, vc * bc[..., None])
    w = jnp.einsum('bnhij,bnhjd->bnhid', tmat, kb * jnp.exp(gc)[..., None])
    a_in = jnp.einsum('bnhid,bnhjd->bnhij', qc, kc) * decay
    qg = qc * jnp.exp(gc)[..., None]
    g_last = gc[..., -1]
    kt = kc * jnp.exp(g_last[..., None] - gc)[..., None]

    def step(S, inp):
        qg_i, w_i, u_i, a_i, eg_i, kt_i = inp
        v_new = u_i - jnp.einsum('bhck,bhkv->bhcv', w_i, S)
        o_i = jnp.einsum('bhck,bhkv->bhcv', qg_i, S) + jnp.einsum('bhij,bhjv->bhiv', a_i, v_new)
        S = S * eg_i[..., None, None] + jnp.einsum('bhck,bhcv->bhkv', kt_i, v_new)
        return S, o_i

    xs = tuple(jnp.moveaxis(a, 1, 0) for a in (qg, w, u, a_in, jnp.exp(g_last), kt))
    S0 = jnp.zeros((B, H, k.shape[-1], v.shape[-1]), jnp.float32)
    S, o = lax.scan(step, S0, xs)
    o = jnp.transpose(o, (1, 0, 3, 2, 4)).reshape(B, T, H, v.shape[-1])
    return o, S


def _gdn_recurrent(S0, q, k, v, g, beta):
    def step(S, inp):
        qt, kt, vt, gt, bt = inp
        S = S * jnp.exp(gt)[..., None, None]
        pred = jnp.einsum('bhk,bhkv->bhv', kt, S)
        S = S + jnp.einsum('bhk,bhv->bhkv', kt, (vt - pred) * bt[..., None])
        return S, jnp.einsum('bhk,bhkv->bhv', qt, S)

    xs = tuple(jnp.moveaxis(a, 1, 0) for a in (q, k, v, g, beta))
    S, o = lax.scan(step, S0, xs)
    return jnp.moveaxis(o, 0, 1), S


def _project(h, w_in_l, pos):
    B, T, _ = h.shape
    q, k, v, qkv_d, z, b_raw, a_raw = _split_proj(h @ w_in_l)
    q = _rope(q.reshape(B, T, H_A, 2, HD_C), pos)
    k = _rope(k.reshape(B, T, H_A, 2, HD_C), pos)
    v = v.reshape(B, T, H_A, HD_V)
    return q, k, v, qkv_d, z, b_raw, a_raw


def _merge(att_o, gdn_o, z, lam_init, subln_g_l, gdn_norm_g_l, w_out_l):
    B, T = att_o.shape[:2]
    a = _rmsnorm(att_o, subln_g_l) * (1.0 - lam_init)
    f = jnp.float32
    d = gdn_o * lax.rsqrt(jnp.mean(gdn_o * gdn_o, axis=-1, keepdims=True) + EPS) * gdn_norm_g_l.astype(f)
    d = d * jax.nn.silu(z.astype(f).reshape(B, T, H_D, DV))
    y = jnp.concatenate([a.reshape(B, T, -1), d.reshape(B, T, -1).astype(att_o.dtype)], axis=-1)
    return y @ w_out_l


def _swiglu(h, wg, wu, wd):
    return (jax.nn.silu(h @ wg) * (h @ wu)) @ wd


def setup_inputs(seed: int = 0) -> dict:
    key = jax.random.key(seed)
    ks = iter(jax.random.split(key, 40))
    f = jnp.float32

    def nrm(shape, s):
        return jax.random.normal(next(ks), shape, f) * s

    n_pages = PAST_LEN // PAGE_SIZE
    n_used = DEC_BATCH * n_pages
    n_pool = -(-5 * n_used // 4)
    page_table = jax.random.permutation(next(ks), n_pool)[:n_used].reshape(DEC_BATCH, n_pages).astype(jnp.int32)
    sd = D_MODEL ** -0.5
    dt = jnp.exp(jax.random.uniform(next(ks), (DEPTH, H_D), f, math.log(1e-3), math.log(1e-1)))
    return {
        "x_prompt": nrm((BATCH, SEQ, D_MODEL), 1.0),
        "x_sample": nrm((DEC_BATCH, DEC_SEQ, D_MODEL), 1.0),
        "cache_k": nrm((DEPTH, n_pool, PAGE_SIZE, H_A, HD_QK), 1.0),
        "cache_v": nrm((DEPTH, n_pool, PAGE_SIZE, H_A, HD_V), 1.0),
        "state_conv": nrm((DEPTH, DEC_BATCH, CONV - 1, CONV_CH), 1.0),
        "state_ssm": nrm((DEPTH, DEC_BATCH, H_D, DK, DV), 0.1),
        "page_table": page_table,
        "c_prompt": nrm((BATCH, D_MODEL), 1.0),
        "c_sample": nrm((DEC_BATCH, D_MODEL), 1.0),
        "w_mod": nrm((DEPTH, D_MODEL, N_MOD * D_MODEL), 0.5 * sd),
        "b_mod": nrm((DEPTH, N_MOD * D_MODEL), 0.01),
        "norm_mix_g": 1.0 + nrm((DEPTH, D_MODEL), 0.02),
        "w_in": jnp.concatenate([nrm((DEPTH, D_MODEL, N_IN - H_D), sd),
                                 nrm((DEPTH, D_MODEL, H_D), 0.1 * sd)], axis=-1),
        "conv_w": nrm((DEPTH, CONV, CONV_CH), 0.5),
        "a_log": jnp.log(jax.random.uniform(next(ks), (DEPTH, H_D), f, 1.0, 16.0)),
        "dt_bias": dt + jnp.log(-jnp.expm1(-dt)),
        "lambda_q1": nrm((DEPTH, HD_C), 0.1),
        "lambda_k1": nrm((DEPTH, HD_C), 0.1),
        "lambda_q2": nrm((DEPTH, HD_C), 0.1),
        "lambda_k2": nrm((DEPTH, HD_C), 0.1),
        "subln_g": 1.0 + nrm((DEPTH, HD_V), 0.02),
        "gdn_norm_g": 1.0 + nrm((DEPTH, DV), 0.02),
        "w_out": nrm((DEPTH, D_MODEL, D_MODEL), sd),
        "norm_ffn_g": 1.0 + nrm((DEPTH, D_MODEL), 0.02),
        "w_gate": nrm((DEPTH, D_MODEL, D_FF), sd),
        "w_up": nrm((DEPTH, D_MODEL, D_FF), sd),
        "w_down": nrm((DEPTH, D_FF, D_MODEL), D_FF ** -0.5),
        "final_norm_g": 1.0 + nrm((D_MODEL,), 0.02),
    }


def reference(x_prompt, x_sample, cache_k, cache_v, state_conv, state_ssm, page_table,
              c_prompt, c_sample, w_mod, b_mod, norm_mix_g, w_in, conv_w, a_log, dt_bias,
              lambda_q1, lambda_k1, lambda_q2, lambda_k2, subln_g, gdn_norm_g, w_out,
              norm_ffn_g, w_gate, w_up, w_down, final_norm_g):
    B, T, _ = x_prompt.shape
    DB, DS, _ = x_sample.shape
    past = page_table.shape[1] * cache_k.shape[2]
    pos_p = jnp.arange(T)
    pos_s = past + jnp.arange(DS)
    xp, xs = x_prompt, x_sample
    kp_l, vp_l, cp_l, sp_l, ks_l, vs_l, cs_l, ss_l = [], [], [], [], [], [], [], []
    for l in range(DEPTH):
        lam_init = 0.8 - 0.6 * math.exp(-0.3 * l)
        lam = _lambda(lambda_q1[l], lambda_k1[l], lambda_q2[l], lambda_k2[l], lam_init)
        mp = _adaln(c_prompt, w_mod[l], b_mod[l])
        ms = _adaln(c_sample, w_mod[l], b_mod[l])

        h = _modulate(_rmsnorm(xp, norm_mix_g[l]), mp[0], mp[1])
        q, k, v, cin, z, br, ar = _project(h, w_in[l], pos_p)
        att = _diff_attn_prompt(q, k, v, lam)
        cpad = jnp.concatenate([jnp.zeros((B, CONV - 1, CONV_CH), cin.dtype), cin], axis=1)
        qd, kd, vd, gd, bd = _gdn_prep(_causal_conv(cpad, conv_w[l], T), br, ar, a_log[l], dt_bias[l])
        od, S_p = _gdn_chunked(qd, kd, vd, gd, bd)
        xp = xp + mp[2] * _merge(att, od, z, lam_init, subln_g[l], gdn_norm_g[l], w_out[l])
        kp_l.append(k.reshape(B, T, H_A, HD_QK))
        vp_l.append(v)
        cp_l.append(cpad[:, -(CONV - 1):])
        sp_l.append(S_p.astype(xp.dtype))
        h = _modulate(_rmsnorm(xp, norm_ffn_g[l]), mp[3], mp[4])
        xp = xp + mp[5] * _swiglu(h, w_gate[l], w_up[l], w_down[l])

        h = _modulate(_rmsnorm(xs, norm_mix_g[l]), ms[0], ms[1])
        q, k, v, cin, z, br, ar = _project(h, w_in[l], pos_s)
        att = _diff_attn_sample(q, k, v, cache_k[l], cache_v[l], page_table, lam)
        cpad = jnp.concatenate([state_conv[l].astype(cin.dtype), cin], axis=1)
        qd, kd, vd, gd, bd = _gdn_prep(_causal_conv(cpad, conv_w[l], DS), br, ar, a_log[l], dt_bias[l])
        od, S_s = _gdn_recurrent(state_ssm[l].astype(jnp.float32), qd, kd, vd, gd, bd)
        xs = xs + ms[2] * _merge(att, od, z, lam_init, subln_g[l], gdn_norm_g[l], w_out[l])
        ks_l.append(k.reshape(DB, DS, H_A, HD_QK))
        vs_l.append(v)
        cs_l.append(cpad[:, -(CONV - 1):])
        ss_l.append(S_s.astype(xs.dtype))
        h = _modulate(_rmsnorm(xs, norm_ffn_g[l]), ms[3], ms[4])
        xs = xs + ms[5] * _swiglu(h, w_gate[l], w_up[l], w_down[l])

    y_prompt = _rmsnorm(xp, final_norm_g)
    y_sample = _rmsnorm(xs, final_norm_g)
    k_prompt = jnp.stack(kp_l)
    v_prompt = jnp.stack(vp_l)
    conv_prompt = jnp.stack(cp_l)
    ssm_prompt = jnp.stack(sp_l)
    k_sample = jnp.stack(ks_l)
    v_sample = jnp.stack(vs_l)
    conv_sample = jnp.stack(cs_l)
    ssm_sample = jnp.stack(ss_l)
    return (y_prompt, y_sample, k_prompt, v_prompt, conv_prompt, ssm_prompt,
            k_sample, v_sample, conv_sample, ssm_sample)
```

```python
import functools
import math

import jax
import jax.numpy as jnp
import numpy as np
from jax import lax
from jax.experimental import pallas as pl
from jax.experimental.pallas import tpu as pltpu

F32 = jnp.float32
BF16 = jnp.bfloat16

HD_V = 128
HD_C = 64
HD_QK = 2 * HD_C
DK = 128
DV = 128
ROT_DIM = HD_C // 4
ROT_HALF = ROT_DIM // 2
ROPE_THETA = 500000.0
CONV = 4
GDN_CHUNK = 64
N_MOD = 6
EPS = 1e-6
MOD_ROWS = 32
SAMPLE_ROW0 = 16

VMEM_LIMIT_BYTES = 56 * 1024 * 1024
LANES = 128


def _params(*sem):
    return pltpu.CompilerParams(dimension_semantics=sem, vmem_limit_bytes=VMEM_LIMIT_BYTES)


def _tile(dim, want):
    if dim <= want:
        return dim
    t = want
    while dim % t:
        t -= 8
    return t


def _sigmoid(x):
    return 1.0 / (1.0 + jnp.exp(-x))


def _silu(x):
    return x * _sigmoid(x)


def _softplus(x):
    return jnp.maximum(x, 0.0) + jnp.log(1.0 + jnp.exp(-jnp.abs(x)))


def _adaln_kernel(c_ref, w_ref, b_ref, o_ref):
    acc = jnp.dot(c_ref[...].astype(BF16), w_ref[...].astype(BF16), preferred_element_type=F32)
    o_ref[...] = acc + b_ref[...]


def adaln(c_all, w_mod, b_mod):
    depth, d, n = w_mod.shape
    tn = _tile(n, 512)
    return pl.pallas_call(
        _adaln_kernel,
        grid=(depth, n // tn),
        in_specs=[
            pl.BlockSpec((MOD_ROWS, d), lambda l, j: (0, 0)),
            pl.BlockSpec((None, d, tn), lambda l, j: (l, 0, j)),
            pl.BlockSpec((None, 1, tn), lambda l, j: (l, 0, j)),
        ],
        out_specs=pl.BlockSpec((None, MOD_ROWS, tn), lambda l, j: (l, 0, j)),
        out_shape=jax.ShapeDtypeStruct((depth, MOD_ROWS, n), F32),
        compiler_params=_params("parallel", "parallel"),
        name="adaln",
    )(c_all, w_mod, b_mod.reshape(depth, 1, n))


def _norm_mod_kernel(x_ref, g_ref, shift_ref, scale_ref, o_ref):
    x = x_ref[...]
    y = x * lax.rsqrt(jnp.mean(x * x, axis=-1, keepdims=True) + EPS) * g_ref[...]
    o_ref[...] = (y * (1.0 + scale_ref[...]) + shift_ref[...]).astype(o_ref.dtype)


def _norm_kernel(x_ref, g_ref, o_ref):
    x = x_ref[...]
    o_ref[...] = (x * lax.rsqrt(jnp.mean(x * x, axis=-1, keepdims=True) + EPS) * g_ref[...]).astype(o_ref.dtype)


def norm_mod_prompt(x, g, mods3, layer, i_shift, i_scale, seq):
    m, d = x.shape
    tm = _tile(seq, 256)
    per = seq // tm

    def mod_idx(which):
        return lambda i: ((layer * MOD_ROWS + i // per) * N_MOD + which, 0, 0)

    return pl.pallas_call(
        _norm_mod_kernel,
        grid=(m // tm,),
        in_specs=[
            pl.BlockSpec((tm, d), lambda i: (i, 0)),
            pl.BlockSpec((1, d), lambda i: (0, 0)),
            pl.BlockSpec((None, 1, d), mod_idx(i_shift)),
            pl.BlockSpec((None, 1, d), mod_idx(i_scale)),
        ],
        out_specs=pl.BlockSpec((tm, d), lambda i: (i, 0)),
        out_shape=jax.ShapeDtypeStruct((m, d), BF16),
        compiler_params=_params("parallel"),
        name="norm_mod_prompt",
    )(x, g.reshape(1, d), mods3, mods3)


def norm_mod_sample(x, g, shift, scale):
    m, d = x.shape
    full = pl.BlockSpec((m, d), lambda: (0, 0))
    return pl.pallas_call(
        _norm_mod_kernel,
        in_specs=[full, pl.BlockSpec((1, d), lambda: (0, 0)), full, full],
        out_specs=full,
        out_shape=jax.ShapeDtypeStruct((m, d), BF16),
        name="norm_mod_sample",
    )(x, g.reshape(1, d), shift, scale)


def final_norm(x, g):
    m, d = x.shape
    tm = _tile(m, 256)
    return pl.pallas_call(
        _norm_kernel,
        grid=(m // tm,),
        in_specs=[pl.BlockSpec((tm, d), lambda i: (i, 0)), pl.BlockSpec((1, d), lambda i: (0, 0))],
        out_specs=pl.BlockSpec((tm, d), lambda i: (i, 0)),
        out_shape=jax.ShapeDtypeStruct((m, d), F32),
        compiler_params=_params("parallel"),
        name="final_norm",
    )(x, g.reshape(1, d))


def _mm_kernel(a_ref, w_ref, o_ref):
    o_ref[...] = jnp.dot(a_ref[...], w_ref[...], preferred_element_type=F32).astype(o_ref.dtype)


def matmul(a, w, n_out, tm_want=1024, tn_want=1024):
    m, k = a.shape
    tm, tn = _tile(m, tm_want), _tile(n_out, tn_want)
    return pl.pallas_call(
        _mm_kernel,
        grid=(m // tm, n_out // tn),
        in_specs=[pl.BlockSpec((tm, k), lambda i, j: (i, 0)), pl.BlockSpec((k, tn), lambda i, j: (0, j))],
        out_specs=pl.BlockSpec((tm, tn), lambda i, j: (i, j)),
        out_shape=jax.ShapeDtypeStruct((m, n_out), F32),
        compiler_params=_params("parallel", "parallel"),
        name="matmul",
    )(a, w)


def _mm_res_kernel(a_ref, w_ref, x_ref, gate_ref, o_ref):
    acc = jnp.dot(a_ref[...], w_ref[...], preferred_element_type=F32)
    o_ref[...] = x_ref[...] + gate_ref[...] * acc


def _mm2_res_kernel(a_ref, d_ref, wa_ref, wd_ref, x_ref, gate_ref, o_ref):
    acc = jnp.dot(a_ref[...], wa_ref[...], preferred_element_type=F32)
    acc = acc + jnp.dot(d_ref[...], wd_ref[...], preferred_element_type=F32)
    o_ref[...] = x_ref[...] + gate_ref[...] * acc


def _gate_spec_prompt(tn, layer, which, per):
    return pl.BlockSpec((None, 1, tn), lambda i, j: ((layer * MOD_ROWS + i // per) * N_MOD + which, 0, j))


def matmul_residual(a, w, x, gate_spec, gate, tm_want, tn_want):
    m, k = a.shape
    n = w.shape[1]
    tm, tn = _tile(m, tm_want), _tile(n, tn_want)
    return pl.pallas_call(
        _mm_res_kernel,
        grid=(m // tm, n // tn),
        in_specs=[
            pl.BlockSpec((tm, k), lambda i, j: (i, 0)),
            pl.BlockSpec((k, tn), lambda i, j: (0, j)),
            pl.BlockSpec((tm, tn), lambda i, j: (i, j)),
            gate_spec(tm, tn),
        ],
        out_specs=pl.BlockSpec((tm, tn), lambda i, j: (i, j)),
        out_shape=jax.ShapeDtypeStruct((m, n), F32),
        compiler_params=_params("parallel", "parallel"),
        name="matmul_residual",
    )(a, w, x, gate)


def matmul2_residual(a, d, w, x, gate_spec, gate, tm_want, tn_want):
    m, kh = a.shape
    n = w.shape[1]
    tm, tn = _tile(m, tm_want), _tile(n, tn_want)
    return pl.pallas_call(
        _mm2_res_kernel,
        grid=(m // tm, n // tn),
        in_specs=[
            pl.BlockSpec((tm, kh), lambda i, j: (i, 0)),
            pl.BlockSpec((tm, kh), lambda i, j: (i, 0)),
            pl.BlockSpec((kh, tn), lambda i, j: (0, j)),
            pl.BlockSpec((kh, tn), lambda i, j: (1, j)),
            pl.BlockSpec((tm, tn), lambda i, j: (i, j)),
            gate_spec(tm, tn),
        ],
        out_specs=pl.BlockSpec((tm, tn), lambda i, j: (i, j)),
        out_shape=jax.ShapeDtypeStruct((m, n), F32),
        compiler_params=_params("parallel", "parallel"),
        name="matmul2_residual",
    )(a, d, w, w, x, gate)


def _gateup_kernel(h_ref, wg_ref, wu_ref, o_ref):
    h = h_ref[...]
    g = jnp.dot(h, wg_ref[...], preferred_element_type=F32)
    u = jnp.dot(h, wu_ref[...], preferred_element_type=F32)
    o_ref[...] = (_silu(g) * u).astype(o_ref.dtype)


def gate_up(h, wg, wu, tm_want=1024, tn_want=256):
    m, k = h.shape
    n = wg.shape[1]
    tm, tn = _tile(m, tm_want), _tile(n, tn_want)
    return pl.pallas_call(
        _gateup_kernel,
        grid=(m // tm, n // tn),
        in_specs=[
            pl.BlockSpec((tm, k), lambda i, j: (i, 0)),
            pl.BlockSpec((k, tn), lambda i, j: (0, j)),
            pl.BlockSpec((k, tn), lambda i, j: (0, j)),
        ],
        out_specs=pl.BlockSpec((tm, tn), lambda i, j: (i, j)),
        out_shape=jax.ShapeDtypeStruct((m, n), BF16),
        compiler_params=_params("parallel", "parallel"),
        name="gate_up",
    )(h, wg, wu)


def _gates_kernel(h_ref, w_ref, wt_ref, pcol_ref, prow_ref, col_ref, row_ref, *, n_heads, chunk):
    h = h_ref[...]
    col = jnp.dot(h, w_ref[...], preferred_element_type=F32)
    row = lax.dot_general(wt_ref[...], h, (((1,), (1,)), ((), ())), preferred_element_type=F32)

    def gate(x, is_beta, a_log, dt_bias):
        return jnp.where(is_beta, _sigmoid(x), -jnp.exp(a_log) * _softplus(x + dt_bias))

    lane = lax.broadcasted_iota(jnp.int32, col.shape, 1)
    col = gate(col, lane < n_heads, pcol_ref[0:1, :], pcol_ref[1:2, :])
    sub = lax.broadcasted_iota(jnp.int32, row.shape, 0)
    row = gate(row, sub < n_heads, prow_ref[:, 0:1], prow_ref[:, 1:2])
    if chunk:
        t_col = lax.broadcasted_iota(jnp.int32, col.shape, 0) % chunk
        t_row = lax.broadcasted_iota(jnp.int32, row.shape, 1) % chunk
        s = 1
        while s < chunk:
            col = col + jnp.where((t_col >= s) & (lane >= n_heads), pltpu.roll(col, s, axis=0), 0.0)
            row = row + jnp.where((t_row >= s) & (sub >= n_heads), pltpu.roll(row, s, axis=1), 0.0)
            s *= 2
    col_ref[...] = col
    row_ref[...] = row


def gates(h, w_ba, a_log, dt_bias, chunk):
    m, k = h.shape
    n2 = w_ba.shape[1]
    nh = n2 // 2
    tm = _tile(m, 512)
    zeros = jnp.zeros((nh,), F32)
    pcol = jnp.stack([jnp.concatenate([zeros, a_log]), jnp.concatenate([zeros, dt_bias])])
    return pl.pallas_call(
        functools.partial(_gates_kernel, n_heads=nh, chunk=chunk),
        grid=(m // tm,),
        in_specs=[
            pl.BlockSpec((tm, k), lambda i: (i, 0)),
            pl.BlockSpec((k, n2), lambda i: (0, 0)),
            pl.BlockSpec((n2, k), lambda i: (0, 0)),
            pl.BlockSpec((2, n2), lambda i: (0, 0)),
            pl.BlockSpec((n2, 2), lambda i: (0, 0)),
        ],
        out_specs=[pl.BlockSpec((tm, n2), lambda i: (i, 0)), pl.BlockSpec((n2, tm), lambda i: (0, i))],
        out_shape=[jax.ShapeDtypeStruct((m, n2), F32), jax.ShapeDtypeStruct((n2, m), F32)],
        compiler_params=_params("parallel"),
        name="gates",
    )(h, w_ba, w_ba.T, pcol, pcol.T)


def _rope_tables(pos, rows):
    inv = ROPE_THETA ** (-jnp.arange(ROT_HALF, dtype=F32) * (2.0 / ROT_DIM))
    ang = pos.astype(F32)[:, None] * inv[None, :]
    cos, sin = jnp.cos(ang), jnp.sin(ang)
    n = pos.shape[0]
    one = jnp.ones((n, HD_C - ROT_DIM), F32)
    zero = jnp.zeros((n, HD_C - ROT_DIM), F32)
    zh = jnp.zeros((n, ROT_HALF), F32)
    c = jnp.concatenate([cos, cos, one], axis=1)
    sa = jnp.concatenate([-sin, zh, zero], axis=1)
    sb = jnp.concatenate([zh, sin, zero], axis=1)
    tabs = [jnp.concatenate([t, t], axis=1) for t in (c, sa, sb)]
    if n != rows:
        tabs = [jnp.broadcast_to(t, (rows, 2 * HD_C)) for t in tabs]
    return tabs


def _qkv_post_kernel(q_ref, k_ref, v_ref, c_ref, sa_ref, sb_ref, qb_ref, kf_ref, kb_ref, vf_ref, vb_ref):
    c, sa, sb = c_ref[...], sa_ref[...], sb_ref[...]

    def rope(x):
        return x * c + pltpu.roll(x, LANES - ROT_HALF, axis=1) * sa + pltpu.roll(x, ROT_HALF, axis=1) * sb

    q = rope(q_ref[...])
    k = rope(k_ref[...])
    qb_ref[...] = (q * (HD_C ** -0.5)).astype(BF16)
    kf_ref[...] = k
    kb_ref[...] = k.astype(BF16)
    v = v_ref[...]
    vf_ref[...] = v
    vb_ref[...] = v.astype(BF16)


def qkv_post(proj, tabs, n_heads, rows_per_seq):
    m = proj.shape[0]
    tm = _tile(rows_per_seq, 1024)
    per = rows_per_seq // tm
    width = n_heads * HD_QK
    blk = lambda off: pl.BlockSpec((tm, HD_QK), lambda i, h, off=off: (i, off + h))
    tab = pl.BlockSpec((tm, HD_QK), lambda i, h: (i % per, 0))
    out = pl.BlockSpec((tm, HD_QK), lambda i, h: (i, h))
    return pl.pallas_call(
        _qkv_post_kernel,
        grid=(m // tm, n_heads),
        in_specs=[blk(0), blk(n_heads), blk(2 * n_heads), tab, tab, tab],
        out_specs=[out] * 5,
        out_shape=[jax.ShapeDtypeStruct((m, width), dt) for dt in (BF16, F32, BF16, F32, BF16)],
        compiler_params=_params("parallel", "parallel"),
        name="qkv_post",
    )(proj, proj, proj, *tabs)


def _lambda_value(lam_ref, lam_init):
    l1 = jnp.sum(lam_ref[0:1, :] * lam_ref[1:2, :], axis=1, keepdims=True)
    l2 = jnp.sum(lam_ref[2:3, :] * lam_ref[3:4, :], axis=1, keepdims=True)
    return jnp.exp(l1) - jnp.exp(l2) + lam_init


def _subln(o, g, lam_init):
    y = o * lax.rsqrt(jnp.mean(o * o, axis=-1, keepdims=True) + EPS) * g
    return y * (1.0 - lam_init)


def _split_maps(q):
    lane = lax.broadcasted_iota(jnp.int32, q.shape, 1)
    zero = jnp.zeros_like(q)
    return jnp.concatenate([jnp.where(lane < HD_C, q, zero), jnp.where(lane >= HD_C, q, zero)], axis=0)


def _flash_kernel(q_ref, k_ref, v_ref, lam_ref, g_ref, o_ref, q2_scr, m_scr, l_scr, acc_scr, *, tq, tk, lam_init):
    qi, ki = pl.program_id(2), pl.program_id(3)
    last_k = (qi * tq + tq - 1) // tk

    @pl.when(ki == 0)
    def _():
        q2_scr[...] = _split_maps(q_ref[...])
        m_scr[...] = jnp.full(m_scr.shape, -jnp.inf, F32)
        l_scr[...] = jnp.zeros(l_scr.shape, F32)
        acc_scr[...] = jnp.zeros(acc_scr.shape, F32)

    @pl.when(ki <= last_k)
    def _():
        s = lax.dot_general(q2_scr[...], k_ref[...], (((1,), (1,)), ((), ())), preferred_element_type=F32)
        row = lax.broadcasted_iota(jnp.int32, s.shape, 0) % tq + qi * tq
        col = lax.broadcasted_iota(jnp.int32, s.shape, 1) + ki * tk
        s = jnp.where(col <= row, s, -jnp.inf)
        m_prev = m_scr[...]
        m_next = jnp.maximum(m_prev, jnp.max(s, axis=1, keepdims=True))
        alpha = jnp.exp(m_prev - m_next)
        p = jnp.exp(s - m_next)
        l_scr[...] = alpha * l_scr[...] + jnp.sum(p, axis=1, keepdims=True)
        acc_scr[...] = alpha * acc_scr[...] + jnp.dot(p.astype(BF16), v_ref[...], preferred_element_type=F32)
        m_scr[...] = m_next

    @pl.when(ki == last_k)
    def _():
        o2 = acc_scr[...] / l_scr[...]
        lam = _lambda_value(lam_ref, lam_init)
        o = o2[:tq] - lam * o2[tq:]
        o_ref[...] = _subln(o, g_ref[...], lam_init).astype(o_ref.dtype)


def diff_attn_prompt(qb, kb, vb, lam_vecs, subln_g, lam_init, batch, seq, n_heads, tq=512, tk=512):
    tq, tk = _tile(seq, tq), _tile(seq, tk)
    nq, nk = seq // tq, seq // tk

    def kv_idx(b, h, qi, ki):
        return (b * nk + jnp.minimum(ki, (qi * tq + tq - 1) // tk), h)

    return pl.pallas_call(
        functools.partial(_flash_kernel, tq=tq, tk=tk, lam_init=lam_init),
        grid=(batch, n_heads, nq, nk),
        in_specs=[
            pl.BlockSpec((tq, HD_QK), lambda b, h, qi, ki: (b * nq + qi, h)),
            pl.BlockSpec((tk, HD_QK), kv_idx),
            pl.BlockSpec((tk, HD_V), kv_idx),
            pl.BlockSpec((4, HD_C), lambda b, h, qi, ki: (0, 0)),
            pl.BlockSpec((1, HD_V), lambda b, h, qi, ki: (0, 0)),
        ],
        out_specs=pl.BlockSpec((tq, HD_V), lambda b, h, qi, ki: (b * nq + qi, h)),
        out_shape=jax.ShapeDtypeStruct((batch * seq, n_heads * HD_V), BF16),
        scratch_shapes=[
            pltpu.VMEM((2 * tq, HD_QK), BF16),
            pltpu.VMEM((2 * tq, 1), F32),
            pltpu.VMEM((2 * tq, 1), F32),
            pltpu.VMEM((2 * tq, HD_V), F32),
        ],
        compiler_params=_params("parallel", "parallel", "parallel", "arbitrary"),
        name="diff_attn_prompt",
    )(qb, kb, vb, lam_vecs, subln_g.reshape(1, HD_V))


def _paged_kernel(pt_ref, q_ref, kn_ref, vn_ref, lam_ref, g_ref, *rest, n_heads, pages, lam_init):
    k_refs, v_refs = rest[:pages], rest[pages:2 * pages]
    o_ref, q2_scr, m_scr, l_scr, acc_scr = rest[2 * pages:]
    pg = pl.program_id(1)

    @pl.when(pg == 0)
    def _():
        q2 = _split_maps(q_ref[...])
        q2_scr[...] = q2
        kn = kn_ref[...].astype(BF16).astype(F32)
        vn = vn_ref[...].astype(BF16).astype(F32)
        s_new = jnp.sum(q2.astype(F32) * jnp.concatenate([kn, kn], axis=0), axis=1, keepdims=True)
        m_scr[...] = s_new
        l_scr[...] = jnp.ones(l_scr.shape, F32)
        acc_scr[...] = jnp.concatenate([vn, vn], axis=0)

    q2 = q2_scr[...]
    for kp_ref, vp_ref in zip(k_refs, v_refs):
        kp = kp_ref[...].reshape(-1, HD_QK).astype(BF16)
        vp = vp_ref[...].reshape(-1, HD_V).astype(BF16)
        s = lax.dot_general(q2, kp, (((1,), (1,)), ((), ())), preferred_element_type=F32)
        row_h = lax.broadcasted_iota(jnp.int32, s.shape, 0) % n_heads
        col_h = lax.broadcasted_iota(jnp.int32, s.shape, 1) % n_heads
        s = jnp.where(row_h == col_h, s, -jnp.inf)
        m_prev = m_scr[...]
        m_next = jnp.maximum(m_prev, jnp.max(s, axis=1, keepdims=True))
        alpha = jnp.exp(m_prev - m_next)
        p = jnp.exp(s - m_next)
        l_scr[...] = alpha * l_scr[...] + jnp.sum(p, axis=1, keepdims=True)
        acc_scr[...] = alpha * acc_scr[...] + jnp.dot(p.astype(BF16), vp, preferred_element_type=F32)
        m_scr[...] = m_next

    @pl.when(pg == pl.num_programs(1) - 1)
    def _():
        o2 = acc_scr[...] / l_scr[...]
        lam = _lambda_value(lam_ref, lam_init)
        o = o2[:n_heads] - lam * o2[n_heads:]
        o_ref[...] = _subln(o, g_ref[...], lam_init).astype(o_ref.dtype)


def diff_attn_sample(qb, k_new, v_new, cache_k, cache_v, page_table, layer, lam_vecs, subln_g, lam_init,
                     pages_per_step=8):
    db, n_heads, _ = qb.shape
    n_pages = page_table.shape[1]
    page = cache_k.shape[2]
    pps = pages_per_step
    while n_pages % pps:
        pps -= 1

    def page_spec(i):
        return pl.BlockSpec((None, None, page, n_heads, HD_QK),
                            lambda b, pg, pt, i=i: (layer, pt[b, pg * pps + i], 0, 0, 0))

    row = lambda b, pg, pt: (b, 0, 0)
    grid_spec = pltpu.PrefetchScalarGridSpec(
        num_scalar_prefetch=1,
        grid=(db, n_pages // pps),
        in_specs=[
            pl.BlockSpec((None, n_heads, HD_QK), row),
            pl.BlockSpec((None, n_heads, HD_QK), row),
            pl.BlockSpec((None, n_heads, HD_V), row),
            pl.BlockSpec((4, HD_C), lambda b, pg, pt: (0, 0)),
            pl.BlockSpec((1, HD_V), lambda b, pg, pt: (0, 0)),
        ] + [page_spec(i) for i in range(pps)] * 2,
        out_specs=pl.BlockSpec((None, n_heads, HD_V), row),
        scratch_shapes=[
            pltpu.VMEM((2 * n_heads, HD_QK), BF16),
            pltpu.VMEM((2 * n_heads, 1), F32),
            pltpu.VMEM((2 * n_heads, 1), F32),
            pltpu.VMEM((2 * n_heads, HD_V), F32),
        ],
    )
    return pl.pallas_call(
        functools.partial(_paged_kernel, n_heads=n_heads, pages=pps, lam_init=lam_init),
        grid_spec=grid_spec,
        out_shape=jax.ShapeDtypeStruct((db, n_heads, HD_V), BF16),
        compiler_params=_params("parallel", "arbitrary"),
        name="diff_attn_sample",
    )(page_table, qb, k_new, v_new, lam_vecs, subln_g.reshape(1, HD_V),
      *([cache_k] * pps), *([cache_v] * pps))


def _l2n(x):
    return x * lax.rsqrt(jnp.sum(x * x, axis=-1, keepdims=True) + EPS)


def _gated_norm(o, g, z):
    return o * lax.rsqrt(jnp.mean(o * o, axis=-1, keepdims=True) + EPS) * g * _silu(z)


def _dot_hi(a, b):
    return jnp.dot(a, b, preferred_element_type=F32, precision=lax.Precision.HIGHEST)


def _unit_lower_inverse(low):
    c = low.shape[0]
    i = lax.broadcasted_iota(jnp.int32, (c, c), 0)
    j = lax.broadcasted_iota(jnp.int32, (c, c), 1)
    eye = jnp.where(i == j, 1.0, 0.0).astype(F32)
    same16 = (i // 16) == (j // 16)
    same32 = (i // 32) == (j // 32)
    d = jnp.where(same16, low, 0.0)
    d2 = _dot_hi(d, d)
    d4 = _dot_hi(d2, d2)
    d8 = _dot_hi(d4, d4)
    x = eye - d
    x = x + _dot_hi(x, d2)
    x = x + _dot_hi(x, d4)
    x = x + _dot_hi(x, d8)
    m1 = jnp.where(same32 & jnp.logical_not(same16), low, 0.0)
    x = x - _dot_hi(x, _dot_hi(m1, x))
    m2 = jnp.where(same32, 0.0, low)
    x = x - _dot_hi(x, _dot_hi(m2, x))
    return x


def _causal_conv_block(x, halo, w):
    row8 = lax.broadcasted_iota(jnp.int32, halo.shape, 0)
    out = None
    for jtap in range(CONV):
        s = CONV - 1 - jtap
        if s == 0:
            xs = x
        else:
            rolled = pltpu.roll(x, s, axis=0)
            head = jnp.where(row8 < s, pltpu.roll(halo, s, axis=0), rolled[:8])
            xs = jnp.concatenate([head, rolled[8:]], axis=0)
        term = xs * w[jtap:jtap + 1, :]
        out = term if out is None else out + term
    return out


def _gdn_chunk_kernel(q_ref, k_ref, v_ref, z_ref, wq_ref, wk_ref, wv_ref, gcol_ref, grow_ref, g_ref,
                      d_ref, s_ref, hq_scr, hk_scr, hv_scr, *, hg, tb):
    ti = pl.program_id(2)
    c = GDN_CHUNK

    @pl.when(ti == 0)
    def _():
        hq_scr[...] = jnp.zeros(hq_scr.shape, F32)
        hk_scr[...] = jnp.zeros(hk_scr.shape, F32)
        hv_scr[...] = jnp.zeros(hv_scr.shape, F32)
        s_ref[...] = jnp.zeros(s_ref.shape, F32)

    xq, xk, xv = q_ref[...], k_ref[...], v_ref[...]
    aq = _silu(_causal_conv_block(xq, hq_scr[...], wq_ref[...]))
    ak = _silu(_causal_conv_block(xk, hk_scr[...], wk_ref[...]))
    av = _silu(_causal_conv_block(xv, hv_scr[...], wv_ref[...]))
    hq_scr[...] = xq[tb - 8:]
    hk_scr[...] = xk[tb - 8:]
    hv_scr[...] = xv[tb - 8:]

    ii = lax.broadcasted_iota(jnp.int32, (c, c), 0)
    jj = lax.broadcasted_iota(jnp.int32, (c, c), 1)
    gcol = gcol_ref[...]
    grow = grow_ref[...]
    gnorm = g_ref[...]

    for h in range(hg):
        lanes = slice(h * DK, (h + 1) * DK)
        s_state = s_ref[h]
        for ci in range(tb // c):
            rows = slice(ci * c, (ci + 1) * c)
            q = _l2n(aq[rows, lanes]) * (DK ** -0.5)
            k = _l2n(ak[rows, lanes])
            v = av[rows, lanes]
            beta = gcol[rows, h:h + 1]
            gc_col = gcol[rows, hg + h:hg + h + 1]
            gc_row = grow[hg + h:hg + h + 1, rows]
            decay = jnp.exp(jnp.where(ii >= jj, gc_col - gc_row, -jnp.inf))
            kb = k * beta
            k_b = k.astype(BF16)
            both = lax.dot_general(jnp.concatenate([kb, q], axis=0).astype(BF16), k_b,
                                   (((1,), (1,)), ((), ())), preferred_element_type=F32)
            low = jnp.where(ii > jj, both[:c] * decay, 0.0)
            a_in = both[c:] * decay
            tmat = _unit_lower_inverse(low)
            eg = jnp.exp(gc_col)
            rhs = jnp.concatenate([v * beta, kb * eg], axis=1).astype(BF16)
            uw = jnp.dot(tmat.astype(BF16), rhs, preferred_element_type=F32)
            u, w = uw[:, :DV], uw[:, DV:]
            g_last = gc_col[c - 1:c, :]
            s_b = s_state.astype(BF16)
            ws_qs = jnp.dot(jnp.concatenate([w, q * eg], axis=0).astype(BF16), s_b, preferred_element_type=F32)
            v_new = u - ws_qs[:c]
            v_new_b = v_new.astype(BF16)
            o = ws_qs[c:] + jnp.dot(a_in.astype(BF16), v_new_b, preferred_element_type=F32)
            kt = (k * jnp.exp(g_last - gc_col)).astype(BF16)
            s_state = s_state * jnp.exp(g_last) + lax.dot_general(
                kt, v_new_b, (((0,), (0,)), ((), ())), preferred_element_type=F32)
            d_ref[rows, lanes] = _gated_norm(o, gnorm, z_ref[rows, lanes]).astype(d_ref.dtype)
        s_ref[h] = s_state


def gdn_prompt(proj, col0, conv_w, gcol, grow, gnorm_g, batch, seq, n_heads, hg=4, tb=128):
    tb = _tile(seq, tb)
    nt = seq // tb
    wblk = hg * DK
    ngrp = n_heads // hg
    assert col0 % wblk == 0
    c0 = col0 // wblk

    def xin(off):
        return pl.BlockSpec((tb, wblk), lambda b, g, t, off=off: (b * nt + t, c0 + off * ngrp + g))

    def win(off):
        return pl.BlockSpec((CONV, wblk), lambda b, g, t, off=off: (0, off * ngrp + g))

    return pl.pallas_call(
        functools.partial(_gdn_chunk_kernel, hg=hg, tb=tb),
        grid=(batch, ngrp, nt),
        in_specs=[
            xin(0), xin(1), xin(2), xin(3),
            win(0), win(1), win(2),
            pl.BlockSpec((None, tb, 2 * hg), lambda b, g, t: (g, b * nt + t, 0)),
            pl.BlockSpec((None, 2 * hg, tb), lambda b, g, t: (g, 0, b * nt + t)),
            pl.BlockSpec((1, DV), lambda b, g, t: (0, 0)),
        ],
        out_specs=[
            pl.BlockSpec((tb, wblk), lambda b, g, t: (b * nt + t, g)),
            pl.BlockSpec((None, hg, DK, DV), lambda b, g, t: (b, g, 0, 0)),
        ],
        out_shape=[
            jax.ShapeDtypeStruct((batch * seq, n_heads * DV), BF16),
            jax.ShapeDtypeStruct((batch, n_heads, DK, DV), F32),
        ],
        scratch_shapes=[pltpu.VMEM((8, wblk), F32)] * 3,
        compiler_params=_params("parallel", "parallel", "arbitrary"),
        name="gdn_prompt",
    )(proj, proj, proj, proj, conv_w, conv_w, conv_w, gcol, grow, gnorm_g.reshape(1, DV))


def _gdn_step_kernel(x_ref, st_ref, w_ref, z_ref, gate_ref, g_ref, s_ref, d_ref, so_ref, *, n_heads):
    w = w_ref[...]
    conv = st_ref[0] * w[0]
    for jtap in range(1, CONV - 1):
        conv = conv + st_ref[jtap] * w[jtap]
    conv = conv + x_ref[...] * w[CONV - 1]
    act = _silu(conv)
    q = _l2n(act[:n_heads]) * (DK ** -0.5)
    k = _l2n(act[n_heads:2 * n_heads])
    v = act[2 * n_heads:]
    qt, kt = q.T, k.T
    gate = gate_ref[...]
    outs = []
    for h in range(n_heads):
        k_col, q_col = kt[:, h:h + 1], qt[:, h:h + 1]
        sd = s_ref[h] * jnp.exp(gate[n_heads + h:n_heads + h + 1, :])
        pred = jnp.sum(sd * k_col, axis=0, keepdims=True)
        delta = (v[h:h + 1, :] - pred) * gate[h:h + 1, :]
        s1 = sd + k_col * delta
        so_ref[h] = s1
        outs.append(jnp.sum(s1 * q_col, axis=0, keepdims=True))
    o = jnp.concatenate(outs, axis=0)
    d_ref[...] = _gated_norm(o, g_ref[...], z_ref[...]).astype(d_ref.dtype)


def gdn_sample(x_new, state_conv, layer, conv_w, z, gate, gnorm_g, state_ssm):
    db, ch, _ = x_new.shape
    n_heads = ch // 3
    return pl.pallas_call(
        functools.partial(_gdn_step_kernel, n_heads=n_heads),
        grid=(db,),
        in_specs=[
            pl.BlockSpec((None, ch, DK), lambda b: (b, 0, 0)),
            pl.BlockSpec((None, None, CONV - 1, ch, DK), lambda b: (layer, b, 0, 0, 0)),
            pl.BlockSpec((CONV, ch, DK), lambda b: (0, 0, 0)),
            pl.BlockSpec((None, n_heads, DV), lambda b: (b, 0, 0)),
            pl.BlockSpec((None, 2 * n_heads, 1), lambda b: (b, 0, 0)),
            pl.BlockSpec((1, DV), lambda b: (0, 0)),
            pl.BlockSpec((None, None, n_heads, DK, DV), lambda b: (layer, b, 0, 0, 0)),
        ],
        out_specs=[
            pl.BlockSpec((None, n_heads, DV), lambda b: (b, 0, 0)),
            pl.BlockSpec((None, n_heads, DK, DV), lambda b: (b, 0, 0, 0)),
        ],
        out_shape=[
            jax.ShapeDtypeStruct((db, n_heads, DV), BF16),
            jax.ShapeDtypeStruct((db, n_heads, DK, DV), F32),
        ],
        compiler_params=_params("parallel"),
        name="gdn_sample",
    )(x_new, state_conv, conv_w, z, gate, gnorm_g.reshape(1, DV), state_ssm)


def kernel(x_prompt, x_sample, cache_k, cache_v, state_conv, state_ssm, page_table, c_prompt, c_sample, w_mod, b_mod, norm_mix_g, w_in, conv_w, a_log, dt_bias, lambda_q1, lambda_k1, lambda_q2, lambda_k2, subln_g, gdn_norm_g, w_out, norm_ffn_g, w_gate, w_up, w_down, final_norm_g):
    batch, seq, d = x_prompt.shape
    db, ds, _ = x_sample.shape
    assert ds == 1
    depth = w_in.shape[0]
    n_a = cache_k.shape[3]
    n_d = state_ssm.shape[2]
    conv_ch = state_conv.shape[3]
    past = page_table.shape[1] * cache_k.shape[2]
    m_p = batch * seq
    qkv_w = n_a * HD_QK
    col_conv = 3 * qkv_w
    col_z = col_conv + conv_ch
    col_ba = col_z + n_d * DV
    hg = 4 if n_d % 4 == 0 else n_d

    c_all = jnp.zeros((MOD_ROWS, d), F32).at[:batch].set(c_prompt).at[SAMPLE_ROW0:SAMPLE_ROW0 + db].set(c_sample)
    mods = adaln(c_all, w_mod, b_mod)
    mods3 = mods.reshape(depth * MOD_ROWS * N_MOD, 1, d)
    mods_s = mods.reshape(depth, MOD_ROWS, N_MOD, d)[:, SAMPLE_ROW0:SAMPLE_ROW0 + db]

    tabs_p = _rope_tables(jnp.arange(seq), seq)
    tabs_s = _rope_tables(past + jnp.arange(ds), db)

    xp = x_prompt.reshape(m_p, d)
    xs = x_sample.reshape(db, d)
    outs = [[] for _ in range(8)]
    for l in range(depth):
        lam_init = 0.8 - 0.6 * math.exp(-0.3 * l)
        lam_vecs = jnp.stack([lambda_q1[l], lambda_k1[l], lambda_q2[l], lambda_k2[l]])
        w_in_b = w_in[l].astype(BF16)
        w_ba_b = w_in_b[:, col_ba:]
        w_out_b = w_out[l].astype(BF16)
        w_gate_b = w_gate[l].astype(BF16)
        w_up_b = w_up[l].astype(BF16)
        w_down_b = w_down[l].astype(BF16)
        conv_w3 = conv_w[l].reshape(CONV, conv_ch // DK, DK)

        def gate_p(which):
            return lambda tm, tn: _gate_spec_prompt(tn, l, which, seq // tm)

        h = norm_mod_prompt(xp, norm_mix_g[l], mods3, l, 0, 1, seq)
        proj = matmul(h, w_in_b, col_ba)
        gcol, grow = gates(h, w_ba_b, a_log[l], dt_bias[l], GDN_CHUNK)
        qb, kf, kb, vf, vb = qkv_post(proj, tabs_p, n_a, seq)
        att = diff_attn_prompt(qb, kb, vb, lam_vecs, subln_g[l], lam_init, batch, seq, n_a)
        ngrp = n_d // hg
        gcol_g = jnp.concatenate([gcol[:, :n_d].reshape(m_p, ngrp, hg), gcol[:, n_d:].reshape(m_p, ngrp, hg)],
                                 axis=2).transpose(1, 0, 2)
        grow_g = jnp.concatenate([grow[:n_d].reshape(ngrp, hg, m_p), grow[n_d:].reshape(ngrp, hg, m_p)], axis=1)
        gdn_o, s_p = gdn_prompt(proj, col_conv, conv_w[l], gcol_g, grow_g, gdn_norm_g[l], batch, seq, n_d, hg=hg)
        xp = matmul2_residual(att, gdn_o, w_out_b, xp, gate_p(2), mods3, min(1024, seq), 512)
        outs[0].append(kf.reshape(batch, seq, n_a, HD_QK))
        outs[1].append(vf.reshape(batch, seq, n_a, HD_V))
        outs[2].append(proj.reshape(batch, seq, -1)[:, seq - (CONV - 1):, col_conv:col_z])
        outs[3].append(s_p)
        h = norm_mod_prompt(xp, norm_ffn_g[l], mods3, l, 3, 4, seq)
        act = gate_up(h, w_gate_b, w_up_b)
        xp = matmul_residual(act, w_down_b, xp, gate_p(5), mods3, min(512, seq), 256)

        ms = mods_s[l]

        def gate_s(which):
            g_rows = ms[:, which]
            return (lambda tm, tn: pl.BlockSpec((tm, tn), lambda i, j: (i, j))), g_rows

        h = norm_mod_sample(xs, norm_mix_g[l], ms[:, 0], ms[:, 1])
        proj = matmul(h, w_in_b, col_ba)
        gcol, _ = gates(h, w_ba_b, a_log[l], dt_bias[l], 0)
        qb, kf, _, vf, _ = qkv_post(proj, tabs_s, n_a, db)
        att = diff_attn_sample(qb.reshape(db, n_a, HD_QK), kf.reshape(db, n_a, HD_QK), vf.reshape(db, n_a, HD_V),
                               cache_k, cache_v, page_table, l, lam_vecs, subln_g[l], lam_init)
        cin = proj[:, col_conv:col_z]
        gdn_o, s_s = gdn_sample(cin.reshape(db, conv_ch // DK, DK),
                                state_conv.reshape(depth, db, CONV - 1, conv_ch // DK, DK), l, conv_w3,
                                proj[:, col_z:col_ba].reshape(db, n_d, DV), gcol.reshape(db, 2 * n_d, 1),
                                gdn_norm_g[l], state_ssm)
        spec, g_rows = gate_s(2)
        xs = matmul2_residual(att.reshape(db, n_a * HD_V), gdn_o.reshape(db, n_d * DV), w_out_b, xs, spec, g_rows,
                              db, 1024)
        outs[4].append(kf.reshape(db, ds, n_a, HD_QK))
        outs[5].append(vf.reshape(db, ds, n_a, HD_V))
        outs[6].append(jnp.concatenate([state_conv[l][:, 1:], cin[:, None, :]], axis=1))
        outs[7].append(s_s)
        h = norm_mod_sample(xs, norm_ffn_g[l], ms[:, 3], ms[:, 4])
        act = gate_up(h, w_gate_b, w_up_b, tn_want=256)
        spec, g_rows = gate_s(5)
        xs = matmul_residual(act, w_down_b, xs, spec, g_rows, db, 256)

    y_prompt = final_norm(xp, final_norm_g).reshape(batch, seq, d)
    y_sample = final_norm(xs, final_norm_g).reshape(db, ds, d)
    return (y_prompt, y_sample) + tuple(jnp.stack(o) for o in outs)
```

```python
import functools
import math

import jax
import jax.numpy as jnp
import numpy as np
from jax import lax
from jax.experimental import pallas as pl
from jax.experimental.pallas import tpu as pltpu

F32 = jnp.float32
BF16 = jnp.bfloat16

HD_V = 128
HD_C = 64
HD_QK = 2 * HD_C
DK = 128
DV = 128
ROT_DIM = HD_C // 4
ROT_HALF = ROT_DIM // 2
ROPE_THETA = 500000.0
Q_SCALE = HD_C ** -0.5 * math.log2(math.e)
CONV = 4
GDN_CHUNK = 64
N_MOD = 6
EPS = 1e-6
MOD_ROWS = 32
SAMPLE_ROW0 = 16
GDN_HEADS_PER_STEP = 8
GDN_ROWS_PER_STEP = 128
FLASH_SLAB = 64

VMEM_LIMIT_BYTES = 56 * 1024 * 1024
LANES = 128


def _params(*sem):
    return pltpu.CompilerParams(dimension_semantics=sem, vmem_limit_bytes=VMEM_LIMIT_BYTES)


def _tile(dim, want):
    if dim <= want:
        return dim
    t = want
    while dim % t:
        t -= 8
    return t


def _sigmoid(x):
    return 1.0 / (1.0 + jnp.exp(-x))


def _silu(x):
    return x * _sigmoid(x)


def _softplus(x):
    return jnp.maximum(x, 0.0) + jnp.log(1.0 + jnp.exp(-jnp.abs(x)))


def _adaln_kernel(c_ref, w_ref, b_ref, o_ref):
    acc = jnp.dot(c_ref[...].astype(BF16), w_ref[...].astype(BF16), preferred_element_type=F32)
    o_ref[...] = acc + b_ref[...]


def adaln(c_all, w_mod, b_mod):
    depth, d, n = w_mod.shape
    tn = _tile(n, 512)
    return pl.pallas_call(
        _adaln_kernel,
        grid=(depth, n // tn),
        in_specs=[
            pl.BlockSpec((MOD_ROWS, d), lambda l, j: (0, 0)),
            pl.BlockSpec((None, d, tn), lambda l, j: (l, 0, j)),
            pl.BlockSpec((None, 1, tn), lambda l, j: (l, 0, j)),
        ],
        out_specs=pl.BlockSpec((None, MOD_ROWS, tn), lambda l, j: (l, 0, j)),
        out_shape=jax.ShapeDtypeStruct((depth, MOD_ROWS, n), F32),
        compiler_params=_params("parallel", "parallel"),
        name="adaln",
    )(c_all, w_mod, b_mod.reshape(depth, 1, n))


def _norm_mod_kernel(x_ref, g_ref, shift_ref, scale_ref, o_ref):
    x = x_ref[...]
    y = x * lax.rsqrt(jnp.mean(x * x, axis=-1, keepdims=True) + EPS) * g_ref[...]
    o_ref[...] = (y * (1.0 + scale_ref[...]) + shift_ref[...]).astype(o_ref.dtype)


def _norm_kernel(x_ref, g_ref, o_ref):
    x = x_ref[...]
    o_ref[...] = (x * lax.rsqrt(jnp.mean(x * x, axis=-1, keepdims=True) + EPS) * g_ref[...]).astype(o_ref.dtype)


def norm_mod_prompt(x, g, mods3, layer, i_shift, i_scale, seq):
    m, d = x.shape
    tm = _tile(seq, 256)
    per = seq // tm

    def mod_idx(which):
        return lambda i: ((layer * MOD_ROWS + i // per) * N_MOD + which, 0, 0)

    return pl.pallas_call(
        _norm_mod_kernel,
        grid=(m // tm,),
        in_specs=[
            pl.BlockSpec((tm, d), lambda i: (i, 0)),
            pl.BlockSpec((1, d), lambda i: (0, 0)),
            pl.BlockSpec((None, 1, d), mod_idx(i_shift)),
            pl.BlockSpec((None, 1, d), mod_idx(i_scale)),
        ],
        out_specs=pl.BlockSpec((tm, d), lambda i: (i, 0)),
        out_shape=jax.ShapeDtypeStruct((m, d), BF16),
        compiler_params=_params("parallel"),
        name="norm_mod_prompt",
    )(x, g.reshape(1, d), mods3, mods3)


def norm_mod_sample(x, g, shift, scale):
    m, d = x.shape
    full = pl.BlockSpec((m, d), lambda: (0, 0))
    return pl.pallas_call(
        _norm_mod_kernel,
        in_specs=[full, pl.BlockSpec((1, d), lambda: (0, 0)), full, full],
        out_specs=full,
        out_shape=jax.ShapeDtypeStruct((m, d), BF16),
        name="norm_mod_sample",
    )(x, g.reshape(1, d), shift, scale)


def final_norm(x, g):
    m, d = x.shape
    tm = _tile(m, 256)
    return pl.pallas_call(
        _norm_kernel,
        grid=(m // tm,),
        in_specs=[pl.BlockSpec((tm, d), lambda i: (i, 0)), pl.BlockSpec((1, d), lambda i: (0, 0))],
        out_specs=pl.BlockSpec((tm, d), lambda i: (i, 0)),
        out_shape=jax.ShapeDtypeStruct((m, d), F32),
        compiler_params=_params("parallel"),
        name="final_norm",
    )(x, g.reshape(1, d))


def _lane_tile(n, want):
    best = n
    for t in range(LANES, min(n, want) + 1, LANES):
        if n % t == 0:
            best = t
    return best


def _cast_kernel(w_ref, o_ref):
    o_ref[...] = w_ref[...].astype(o_ref.dtype)


def cast_layer(w, layer, n_cols=None):
    _, k, n = w.shape
    n_cols = n if n_cols is None else n_cols
    tk, tn = _tile(k, 256), _lane_tile(n_cols, 8192)
    return pl.pallas_call(
        _cast_kernel,
        grid=(k // tk, n_cols // tn),
        in_specs=[pl.BlockSpec((None, tk, tn), lambda i, j: (layer, i, j))],
        out_specs=pl.BlockSpec((tk, tn), lambda i, j: (i, j)),
        out_shape=jax.ShapeDtypeStruct((k, n_cols), BF16),
        compiler_params=_params("parallel", "parallel"),
        name="cast_layer",
    )(w)


def _mm_kernel(a_ref, w_ref, o_ref):
    o_ref[...] = jnp.dot(a_ref[...], w_ref[...], preferred_element_type=F32).astype(o_ref.dtype)


def matmul(a, w, n_out, tm_want=1024, tn_want=1024):
    m, k = a.shape
    tm, tn = _tile(m, tm_want), _tile(n_out, tn_want)
    return pl.pallas_call(
        _mm_kernel,
        grid=(m // tm, n_out // tn),
        in_specs=[pl.BlockSpec((tm, k), lambda i, j: (i, 0)), pl.BlockSpec((k, tn), lambda i, j: (0, j))],
        out_specs=pl.BlockSpec((tm, tn), lambda i, j: (i, j)),
        out_shape=jax.ShapeDtypeStruct((m, n_out), F32),
        compiler_params=_params("parallel", "parallel"),
        name="matmul",
    )(a, w)


def _mm_res_kernel(a_ref, w_ref, x_ref, gate_ref, o_ref):
    acc = jnp.dot(a_ref[...], w_ref[...], preferred_element_type=F32)
    o_ref[...] = x_ref[...] + gate_ref[...] * acc


def _mm2_res_kernel(a_ref, d_ref, wa_ref, wd_ref, x_ref, gate_ref, o_ref):
    acc = jnp.dot(a_ref[...], wa_ref[...], preferred_element_type=F32)
    acc = acc + jnp.dot(d_ref[...], wd_ref[...], preferred_element_type=F32)
    o_ref[...] = x_ref[...] + gate_ref[...] * acc


def _gate_spec_prompt(tn, layer, which, per):
    return pl.BlockSpec((None, 1, tn), lambda i, j: ((layer * MOD_ROWS + i // per) * N_MOD + which, 0, j))


def matmul_residual(a, w, x, gate_spec, gate, tm_want, tn_want):
    m, k = a.shape
    n = w.shape[1]
    tm, tn = _tile(m, tm_want), _tile(n, tn_want)
    return pl.pallas_call(
        _mm_res_kernel,
        grid=(m // tm, n // tn),
        in_specs=[
            pl.BlockSpec((tm, k), lambda i, j: (i, 0)),
            pl.BlockSpec((k, tn), lambda i, j: (0, j)),
            pl.BlockSpec((tm, tn), lambda i, j: (i, j)),
            gate_spec(tm, tn),
        ],
        out_specs=pl.BlockSpec((tm, tn), lambda i, j: (i, j)),
        out_shape=jax.ShapeDtypeStruct((m, n), F32),
        compiler_params=_params("parallel", "parallel"),
        name="matmul_residual",
    )(a, w, x, gate)


def matmul2_residual(a, d, w, x, gate_spec, gate, tm_want, tn_want):
    m, kh = a.shape
    n = w.shape[1]
    tm, tn = _tile(m, tm_want), _tile(n, tn_want)
    return pl.pallas_call(
        _mm2_res_kernel,
        grid=(m // tm, n // tn),
        in_specs=[
            pl.BlockSpec((tm, kh), lambda i, j: (i, 0)),
            pl.BlockSpec((tm, kh), lambda i, j: (i, 0)),
            pl.BlockSpec((kh, tn), lambda i, j: (0, j)),
            pl.BlockSpec((kh, tn), lambda i, j: (1, j)),
            pl.BlockSpec((tm, tn), lambda i, j: (i, j)),
            gate_spec(tm, tn),
        ],
        out_specs=pl.BlockSpec((tm, tn), lambda i, j: (i, j)),
        out_shape=jax.ShapeDtypeStruct((m, n), F32),
        compiler_params=_params("parallel", "parallel"),
        name="matmul2_residual",
    )(a, d, w, w, x, gate)


def _gateup_kernel(h_ref, wg_ref, wu_ref, o_ref):
    h = h_ref[...]
    g = jnp.dot(h, wg_ref[...], preferred_element_type=F32)
    u = jnp.dot(h, wu_ref[...], preferred_element_type=F32)
    o_ref[...] = (_silu(g) * u).astype(o_ref.dtype)


def gate_up(h, wg, wu, tm_want=1024, tn_want=256):
    m, k = h.shape
    n = wg.shape[1]
    tm, tn = _tile(m, tm_want), _tile(n, tn_want)
    return pl.pallas_call(
        _gateup_kernel,
        grid=(m // tm, n // tn),
        in_specs=[
            pl.BlockSpec((tm, k), lambda i, j: (i, 0)),
            pl.BlockSpec((k, tn), lambda i, j: (0, j)),
            pl.BlockSpec((k, tn), lambda i, j: (0, j)),
        ],
        out_specs=pl.BlockSpec((tm, tn), lambda i, j: (i, j)),
        out_shape=jax.ShapeDtypeStruct((m, n), BF16),
        compiler_params=_params("parallel", "parallel"),
        name="gate_up",
    )(h, wg, wu)


def _gates_kernel(h_ref, w_ref, wt_ref, pcol_ref, prow_ref, col_ref, row_ref, *, n_heads, chunk):
    h = h_ref[...]
    col = jnp.dot(h, w_ref[...], preferred_element_type=F32)
    row = lax.dot_general(wt_ref[...], h, (((1,), (1,)), ((), ())), preferred_element_type=F32)

    def gate(x, is_beta, a_log, dt_bias):
        return jnp.where(is_beta, _sigmoid(x), -jnp.exp(a_log) * _softplus(x + dt_bias))

    lane = lax.broadcasted_iota(jnp.int32, col.shape, 1)
    col = gate(col, lane < n_heads, pcol_ref[0:1, :], pcol_ref[1:2, :])
    sub = lax.broadcasted_iota(jnp.int32, row.shape, 0)
    row = gate(row, sub < n_heads, prow_ref[:, 0:1], prow_ref[:, 1:2])
    if chunk:
        t_col = lax.broadcasted_iota(jnp.int32, col.shape, 0) % chunk
        t_row = lax.broadcasted_iota(jnp.int32, row.shape, 1) % chunk
        s = 1
        while s < chunk:
            col = col + jnp.where((t_col >= s) & (lane >= n_heads), pltpu.roll(col, s, axis=0), 0.0)
            row = row + jnp.where((t_row >= s) & (sub >= n_heads), pltpu.roll(row, s, axis=1), 0.0)
            s *= 2
    col_ref[...] = col
    row_ref[...] = row


def gates(h, w_ba, a_log, dt_bias, chunk):
    m, k = h.shape
    n2 = w_ba.shape[1]
    nh = n2 // 2
    tm = _tile(m, 512)
    zeros = jnp.zeros((nh,), F32)
    pcol = jnp.stack([jnp.concatenate([zeros, a_log]), jnp.concatenate([zeros, dt_bias])])
    return pl.pallas_call(
        functools.partial(_gates_kernel, n_heads=nh, chunk=chunk),
        grid=(m // tm,),
        in_specs=[
            pl.BlockSpec((tm, k), lambda i: (i, 0)),
            pl.BlockSpec((k, n2), lambda i: (0, 0)),
            pl.BlockSpec((n2, k), lambda i: (0, 0)),
            pl.BlockSpec((2, n2), lambda i: (0, 0)),
            pl.BlockSpec((n2, 2), lambda i: (0, 0)),
        ],
        out_specs=[pl.BlockSpec((tm, n2), lambda i: (i, 0)), pl.BlockSpec((n2, tm), lambda i: (0, i))],
        out_shape=[jax.ShapeDtypeStruct((m, n2), F32), jax.ShapeDtypeStruct((n2, m), F32)],
        compiler_params=_params("parallel"),
        name="gates",
    )(h, w_ba, w_ba.T, pcol, pcol.T)


def _rope_tables(pos, rows):
    inv = ROPE_THETA ** (-jnp.arange(ROT_HALF, dtype=F32) * (2.0 / ROT_DIM))
    ang = pos.astype(F32)[:, None] * inv[None, :]
    cos, sin = jnp.cos(ang), jnp.sin(ang)
    n = pos.shape[0]
    one = jnp.ones((n, HD_C - ROT_DIM), F32)
    zero = jnp.zeros((n, HD_C - ROT_DIM), F32)
    zh = jnp.zeros((n, ROT_HALF), F32)
    c = jnp.concatenate([cos, cos, one], axis=1)
    sa = jnp.concatenate([-sin, zh, zero], axis=1)
    sb = jnp.concatenate([zh, sin, zero], axis=1)
    tabs = [jnp.concatenate([t, t], axis=1) for t in (c, sa, sb)]
    if n != rows:
        tabs = [jnp.broadcast_to(t, (rows, 2 * HD_C)) for t in tabs]
    return tabs


def _qkv_post_kernel(q_ref, k_ref, v_ref, c_ref, sa_ref, sb_ref, qb_ref, kf_ref, kb_ref, vf_ref, vb_ref):
    c, sa, sb = c_ref[...], sa_ref[...], sb_ref[...]

    def rope(x):
        return x * c + pltpu.roll(x, LANES - ROT_HALF, axis=1) * sa + pltpu.roll(x, ROT_HALF, axis=1) * sb

    q = rope(q_ref[...])
    k = rope(k_ref[...])
    qb_ref[...] = (q * Q_SCALE).astype(BF16)
    kf_ref[...] = k
    kb_ref[...] = k.astype(BF16)
    v = v_ref[...]
    vf_ref[...] = v
    vb_ref[...] = v.astype(BF16)


def qkv_post(proj, tabs, n_heads, rows_per_seq):
    m = proj.shape[0]
    tm = _tile(rows_per_seq, 1024)
    per = rows_per_seq // tm
    width = n_heads * HD_QK
    blk = lambda off: pl.BlockSpec((tm, HD_QK), lambda i, h, off=off: (i, off + h))
    tab = pl.BlockSpec((tm, HD_QK), lambda i, h: (i % per, 0))
    out = pl.BlockSpec((tm, HD_QK), lambda i, h: (i, h))
    return pl.pallas_call(
        _qkv_post_kernel,
        grid=(m // tm, n_heads),
        in_specs=[blk(0), blk(n_heads), blk(2 * n_heads), tab, tab, tab],
        out_specs=[out] * 5,
        out_shape=[jax.ShapeDtypeStruct((m, width), dt) for dt in (BF16, F32, BF16, F32, BF16)],
        compiler_params=_params("parallel", "parallel"),
        name="qkv_post",
    )(proj, proj, proj, *tabs)


def _lambda_value(lam_ref, lam_init):
    l1 = jnp.sum(lam_ref[0:1, :] * lam_ref[1:2, :], axis=1, keepdims=True)
    l2 = jnp.sum(lam_ref[2:3, :] * lam_ref[3:4, :], axis=1, keepdims=True)
    return jnp.exp(l1) - jnp.exp(l2) + lam_init


def _subln(o, g, lam_init):
    y = o * lax.rsqrt(jnp.mean(o * o, axis=-1, keepdims=True) + EPS) * g
    return y * (1.0 - lam_init)


def _split_maps(q):
    lane = lax.broadcasted_iota(jnp.int32, q.shape, 1)
    zero = jnp.zeros_like(q)
    return jnp.concatenate([jnp.where(lane < HD_C, q, zero), jnp.where(lane >= HD_C, q, zero)], axis=0)


def _flash_kernel(q_ref, k_ref, v_ref, lam_ref, g_ref, o_ref, *, seq, tq, tk, hq, lam_init):
    lam = _lambda_value(lam_ref, lam_init)
    gain = g_ref[...]
    lanes = [slice(h * HD_QK, (h + 1) * HD_QK) for h in range(hq)]
    nt_dims = (((1,), (1,)), ((), ()))
    tn_dims = (((0,), (0,)), ((), ()))
    key_off = lax.broadcasted_iota(jnp.int32, (tk, 2 * tq), 0)
    qry_off = lax.broadcasted_iota(jnp.int32, (tk, 2 * tq), 1) % tq

    def scores(kb, q2):
        k0 = pl.multiple_of(kb * tk, tk)
        return tuple(lax.dot_general(k_ref[pl.ds(k0, tk), ln], qq, nt_dims, preferred_element_type=F32)
                     for ln, qq in zip(lanes, q2))

    def softmax_pv(kb, m, l, acc, s):
        k0 = pl.multiple_of(kb * tk, tk)
        m_out, l_out, acc_out = [], [], []
        for ln, mh, lh, ah, sh in zip(lanes, m, l, acc, s):
            m_new = jnp.maximum(mh, jnp.max(sh, axis=0, keepdims=True))
            alpha = jnp.exp2(mh - m_new)
            row_sum, slabs = None, []
            for r0 in range(0, tk, FLASH_SLAB):
                pj = jnp.exp2(sh[r0:r0 + FLASH_SLAB] - m_new)
                sj = jnp.sum(pj, axis=0, keepdims=True)
                row_sum = sj if row_sum is None else row_sum + sj
                slabs.append(pj.astype(BF16))
            pv = lax.dot_general(v_ref[pl.ds(k0, tk), ln], jnp.concatenate(slabs, axis=0), tn_dims,
                                 preferred_element_type=F32)
            m_out.append(m_new)
            l_out.append(alpha * lh + row_sum)
            acc_out.append(alpha * ah + pv)
        return tuple(m_out), tuple(l_out), tuple(acc_out)

    def q_block(qb, _):
        q0 = pl.multiple_of(qb * tq, tq)
        q2 = [_split_maps(q_ref[pl.ds(q0, tq), ln]) for ln in lanes]
        n_full = q0 // tk

        def full_step(kb, carry):
            m, l, acc, s = carry
            s_next = scores(kb + 1, q2)
            return softmax_pv(kb, m, l, acc, s) + (s_next,)

        init = (tuple(jnp.full((1, 2 * tq), -jnp.inf, F32) for _ in lanes),
                tuple(jnp.zeros((1, 2 * tq), F32) for _ in lanes),
                tuple(jnp.zeros((HD_V, 2 * tq), F32) for _ in lanes),
                scores(0, q2))
        m, l, acc, s = lax.fori_loop(0, n_full, full_step, init)
        keep = key_off + n_full * tk <= qry_off + q0
        s = [jnp.where(keep, a, -jnp.inf) for a in s]
        _, l, acc = softmax_pv(n_full, m, l, acc, s)
        for ln, lh, ah in zip(lanes, l, acc):
            o2 = ah / lh
            o = (o2[:, :tq] - lam * o2[:, tq:]).T
            o_ref[pl.ds(q0, tq), ln] = _subln(o, gain, lam_init).astype(o_ref.dtype)
        return 0

    lax.fori_loop(0, seq // tq, q_block, 0)


def diff_attn_prompt(qb, kb, vb, lam_vecs, subln_g, lam_init, batch, seq, n_heads, tq=256, tk=256, hq=2):
    tk = _tile(seq, tk)
    tq = _tile(tk, tq)
    hq = math.gcd(n_heads, hq)
    assert tk % tq == 0 and tq % LANES == 0
    blk = pl.BlockSpec((seq, hq * HD_QK), lambda b, g: (b, g))
    return pl.pallas_call(
        functools.partial(_flash_kernel, seq=seq, tq=tq, tk=tk, hq=hq, lam_init=lam_init),
        grid=(batch, n_heads // hq),
        in_specs=[blk, blk, blk,
                  pl.BlockSpec((4, HD_C), lambda b, g: (0, 0)),
                  pl.BlockSpec((1, HD_V), lambda b, g: (0, 0))],
        out_specs=blk,
        out_shape=jax.ShapeDtypeStruct((batch * seq, n_heads * HD_V), BF16),
        compiler_params=_params("parallel", "parallel"),
        name="diff_attn_prompt",
    )(qb, kb, vb, lam_vecs, subln_g.reshape(1, HD_V))


def _paged_kernel(pt_ref, q_ref, kn_ref, vn_ref, lam_ref, g_ref, *rest, n_heads, pages, lam_init):
    k_refs, v_refs = rest[:pages], rest[pages:2 * pages]
    o_ref, q2_scr, m_scr, l_scr, acc_scr = rest[2 * pages:]
    pg = pl.program_id(1)

    @pl.when(pg == 0)
    def _():
        q2 = _split_maps(q_ref[...])
        q2_scr[...] = q2
        kn = kn_ref[...].astype(BF16).astype(F32)
        vn = vn_ref[...].astype(BF16).astype(F32)
        s_new = jnp.sum(q2.astype(F32) * jnp.concatenate([kn, kn], axis=0), axis=1, keepdims=True)
        m_scr[...] = s_new
        l_scr[...] = jnp.ones(l_scr.shape, F32)
        acc_scr[...] = jnp.concatenate([vn, vn], axis=0)

    q2 = q2_scr[...]
    for kp_ref, vp_ref in zip(k_refs, v_refs):
        kp = kp_ref[...].reshape(-1, HD_QK).astype(BF16)
        vp = vp_ref[...].reshape(-1, HD_V).astype(BF16)
        s = lax.dot_general(q2, kp, (((1,), (1,)), ((), ())), preferred_element_type=F32)
        row_h = lax.broadcasted_iota(jnp.int32, s.shape, 0) % n_heads
        col_h = lax.broadcasted_iota(jnp.int32, s.shape, 1) % n_heads
        s = jnp.where(row_h == col_h, s, -jnp.inf)
        m_prev = m_scr[...]
        m_next = jnp.maximum(m_prev, jnp.max(s, axis=1, keepdims=True))
        alpha = jnp.exp2(m_prev - m_next)
        p = jnp.exp2(s - m_next)
        l_scr[...] = alpha * l_scr[...] + jnp.sum(p, axis=1, keepdims=True)
        acc_scr[...] = alpha * acc_scr[...] + jnp.dot(p.astype(BF16), vp, preferred_element_type=F32)
        m_scr[...] = m_next

    @pl.when(pg == pl.num_programs(1) - 1)
    def _():
        o2 = acc_scr[...] / l_scr[...]
        lam = _lambda_value(lam_ref, lam_init)
        o = o2[:n_heads] - lam * o2[n_heads:]
        o_ref[...] = _subln(o, g_ref[...], lam_init).astype(o_ref.dtype)


def diff_attn_sample(qb, k_new, v_new, cache_k, cache_v, page_table, layer, lam_vecs, subln_g, lam_init,
                     pages_per_step=8):
    db, n_heads, _ = qb.shape
    n_pages = page_table.shape[1]
    page = cache_k.shape[2]
    pps = pages_per_step
    while n_pages % pps:
        pps -= 1

    def page_spec(i):
        return pl.BlockSpec((None, None, page, n_heads, HD_QK),
                            lambda b, pg, pt, i=i: (layer, pt[b, pg * pps + i], 0, 0, 0))

    row = lambda b, pg, pt: (b, 0, 0)
    grid_spec = pltpu.PrefetchScalarGridSpec(
        num_scalar_prefetch=1,
        grid=(db, n_pages // pps),
        in_specs=[
            pl.BlockSpec((None, n_heads, HD_QK), row),
            pl.BlockSpec((None, n_heads, HD_QK), row),
            pl.BlockSpec((None, n_heads, HD_V), row),
            pl.BlockSpec((4, HD_C), lambda b, pg, pt: (0, 0)),
            pl.BlockSpec((1, HD_V), lambda b, pg, pt: (0, 0)),
        ] + [page_spec(i) for i in range(pps)] * 2,
        out_specs=pl.BlockSpec((None, n_heads, HD_V), row),
        scratch_shapes=[
            pltpu.VMEM((2 * n_heads, HD_QK), BF16),
            pltpu.VMEM((2 * n_heads, 1), F32),
            pltpu.VMEM((2 * n_heads, 1), F32),
            pltpu.VMEM((2 * n_heads, HD_V), F32),
        ],
    )
    return pl.pallas_call(
        functools.partial(_paged_kernel, n_heads=n_heads, pages=pps, lam_init=lam_init),
        grid_spec=grid_spec,
        out_shape=jax.ShapeDtypeStruct((db, n_heads, HD_V), BF16),
        compiler_params=_params("parallel", "arbitrary"),
        name="diff_attn_sample",
    )(page_table, qb, k_new, v_new, lam_vecs, subln_g.reshape(1, HD_V),
      *([cache_k] * pps), *([cache_v] * pps))


def _l2n(x):
    return x * lax.rsqrt(jnp.sum(x * x, axis=-1, keepdims=True) + EPS)


def _gated_norm(o, g, z):
    return o * lax.rsqrt(jnp.mean(o * o, axis=-1, keepdims=True) + EPS) * g * _silu(z)


def _dot_b(a, b):
    return jnp.dot(a.astype(BF16), b.astype(BF16), preferred_element_type=F32)


def _unit_lower_inverses(lows):
    c = lows[0].shape[0]
    i = lax.broadcasted_iota(jnp.int32, (c, c), 0)
    j = lax.broadcasted_iota(jnp.int32, (c, c), 1)
    eye = jnp.where(i == j, 1.0, 0.0).astype(F32)
    same16 = (i // 16) == (j // 16)
    same32 = (i // 32) == (j // 32)
    off16 = same32 & jnp.logical_not(same16)
    d = [jnp.where(same16, low, 0.0) for low in lows]
    d2 = [_dot_b(a, a) for a in d]
    d4 = [_dot_b(a, a) for a in d2]
    d8 = [_dot_b(a, a) for a in d4]
    x = [eye - a for a in d]
    x = [a + _dot_b(a, p) for a, p in zip(x, d2)]
    x = [a + _dot_b(a, p) for a, p in zip(x, d4)]
    x = [a + _dot_b(a, p) for a, p in zip(x, d8)]
    t = [_dot_b(jnp.where(off16, low, 0.0), a) for low, a in zip(lows, x)]
    x = [a - _dot_b(a, p) for a, p in zip(x, t)]
    t = [_dot_b(jnp.where(same32, 0.0, low), a) for low, a in zip(lows, x)]
    x = [a - _dot_b(a, p) for a, p in zip(x, t)]
    return x


def _causal_conv_block(x, halo, w):
    row8 = lax.broadcasted_iota(jnp.int32, halo.shape, 0)
    out = None
    for jtap in range(CONV):
        s = CONV - 1 - jtap
        if s == 0:
            xs = x
        else:
            rolled = pltpu.roll(x, s, axis=0)
            head = jnp.where(row8 < s, pltpu.roll(halo, s, axis=0), rolled[:8])
            xs = jnp.concatenate([head, rolled[8:]], axis=0)
        term = xs * w[jtap:jtap + 1, :]
        out = term if out is None else out + term
    return out


def _gdn_chunk_kernel(q_ref, k_ref, v_ref, z_ref, wq_ref, wk_ref, wv_ref, gcol_ref, grow_ref, g_ref,
                      d_ref, s_ref, hq_scr, hk_scr, hv_scr, *, hg, tb):
    ti = pl.program_id(2)
    c = GDN_CHUNK

    @pl.when(ti == 0)
    def _():
        hq_scr[...] = jnp.zeros(hq_scr.shape, F32)
        hk_scr[...] = jnp.zeros(hk_scr.shape, F32)
        hv_scr[...] = jnp.zeros(hv_scr.shape, F32)
        s_ref[...] = jnp.zeros(s_ref.shape, F32)

    xq, xk, xv = q_ref[...], k_ref[...], v_ref[...]
    aq = _silu(_causal_conv_block(xq, hq_scr[...], wq_ref[...]))
    ak = _silu(_causal_conv_block(xk, hk_scr[...], wk_ref[...]))
    av = _silu(_causal_conv_block(xv, hv_scr[...], wv_ref[...]))
    hq_scr[...] = xq[tb - 8:]
    hk_scr[...] = xk[tb - 8:]
    hv_scr[...] = xv[tb - 8:]

    ii = lax.broadcasted_iota(jnp.int32, (c, c), 0)
    jj = lax.broadcasted_iota(jnp.int32, (c, c), 1)
    gcol = gcol_ref[...]
    grow = grow_ref[...]
    gnorm = g_ref[...]

    nc = tb // c
    items = [(h, ci) for ci in range(nc) for h in range(hg)]
    lanes = lambda h: slice(h * DK, (h + 1) * DK)
    rows = lambda ci: slice(ci * c, (ci + 1) * c)
    nt_dims = (((1,), (1,)), ((), ()))
    tn_dims = (((0,), (0,)), ((), ()))

    q = [_l2n(aq[rows(ci), lanes(h)]) * (DK ** -0.5) for h, ci in items]
    k = [_l2n(ak[rows(ci), lanes(h)]) for h, ci in items]
    v = [av[rows(ci), lanes(h)] for h, ci in items]
    beta = [gcol[rows(ci), h:h + 1] for h, ci in items]
    gc_col = [gcol[rows(ci), hg + h:hg + h + 1] for h, ci in items]
    gc_row = [grow[hg + h:hg + h + 1, rows(ci)] for h, ci in items]
    decay = [jnp.exp(jnp.where(ii >= jj, a - b, -jnp.inf)) for a, b in zip(gc_col, gc_row)]
    kb = [a * b for a, b in zip(k, beta)]
    both = [lax.dot_general(jnp.concatenate([a, b], axis=0).astype(BF16), kk.astype(BF16), nt_dims,
                            preferred_element_type=F32) for a, b, kk in zip(kb, q, k)]
    low = [jnp.where(ii > jj, a[:c] * dc, 0.0) for a, dc in zip(both, decay)]
    a_in = [(a[c:] * dc).astype(BF16) for a, dc in zip(both, decay)]
    tmat = _unit_lower_inverses(low)
    eg = [jnp.exp(a) for a in gc_col]
    rhs = [jnp.concatenate([vv * b, kk * e], axis=1) for vv, b, kk, e in zip(v, beta, kb, eg)]
    uw = [_dot_b(t, r) for t, r in zip(tmat, rhs)]
    g_last = [a[c - 1:c, :] for a in gc_col]
    wq = [jnp.concatenate([a[:, DV:], qq * e], axis=0).astype(BF16) for a, qq, e in zip(uw, q, eg)]
    kt = [(kk * jnp.exp(gl - a)).astype(BF16) for kk, gl, a in zip(k, g_last, gc_col)]
    e_last = [jnp.exp(gl) for gl in g_last]

    state = [s_ref[h] for h in range(hg)]
    for ci in range(nc):
        sl = slice(ci * hg, (ci + 1) * hg)
        ws_qs = [jnp.dot(a, s.astype(BF16), preferred_element_type=F32) for a, s in zip(wq[sl], state)]
        v_new = [(a[:, :DV] - b[:c]).astype(BF16) for a, b in zip(uw[sl], ws_qs)]
        o = [b[c:] + jnp.dot(a, vn, preferred_element_type=F32) for b, a, vn in zip(ws_qs, a_in[sl], v_new)]
        state = [s * e + lax.dot_general(a, vn, tn_dims, preferred_element_type=F32)
                 for s, e, a, vn in zip(state, e_last[sl], kt[sl], v_new)]
        for h in range(hg):
            d_ref[rows(ci), lanes(h)] = _gated_norm(o[h], gnorm, z_ref[rows(ci), lanes(h)]).astype(d_ref.dtype)
    for h in range(hg):
        s_ref[h] = state[h]


def gdn_prompt(proj, col0, conv_w, gcol, grow, gnorm_g, batch, seq, n_heads, hg, tb=GDN_ROWS_PER_STEP):
    tb = _tile(seq, tb)
    nt = seq // tb
    wblk = hg * DK
    ngrp = n_heads // hg
    assert col0 % wblk == 0
    c0 = col0 // wblk

    def xin(off):
        return pl.BlockSpec((tb, wblk), lambda b, g, t, off=off: (b * nt + t, c0 + off * ngrp + g))

    def win(off):
        return pl.BlockSpec((CONV, wblk), lambda b, g, t, off=off: (0, off * ngrp + g))

    return pl.pallas_call(
        functools.partial(_gdn_chunk_kernel, hg=hg, tb=tb),
        grid=(batch, ngrp, nt),
        in_specs=[
            xin(0), xin(1), xin(2), xin(3),
            win(0), win(1), win(2),
            pl.BlockSpec((None, tb, 2 * hg), lambda b, g, t: (g, b * nt + t, 0)),
            pl.BlockSpec((None, 2 * hg, tb), lambda b, g, t: (g, 0, b * nt + t)),
            pl.BlockSpec((1, DV), lambda b, g, t: (0, 0)),
        ],
        out_specs=[
            pl.BlockSpec((tb, wblk), lambda b, g, t: (b * nt + t, g)),
            pl.BlockSpec((None, hg, DK, DV), lambda b, g, t: (b, g, 0, 0)),
        ],
        out_shape=[
            jax.ShapeDtypeStruct((batch * seq, n_heads * DV), BF16),
            jax.ShapeDtypeStruct((batch, n_heads, DK, DV), F32),
        ],
        scratch_shapes=[pltpu.VMEM((8, wblk), F32)] * 3,
        compiler_params=_params("parallel", "parallel", "arbitrary"),
        name="gdn_prompt",
    )(proj, proj, proj, proj, conv_w, conv_w, conv_w, gcol, grow, gnorm_g.reshape(1, DV))


def _gdn_step_kernel(x_ref, st_ref, w_ref, z_ref, gate_ref, g_ref, s_ref, d_ref, so_ref, *, n_heads):
    w = w_ref[...]
    conv = st_ref[0] * w[0]
    for jtap in range(1, CONV - 1):
        conv = conv + st_ref[jtap] * w[jtap]
    conv = conv + x_ref[...] * w[CONV - 1]
    act = _silu(conv)
    q = _l2n(act[:n_heads]) * (DK ** -0.5)
    k = _l2n(act[n_heads:2 * n_heads])
    v = act[2 * n_heads:]
    qt, kt = q.T, k.T
    gate = gate_ref[...]
    outs = []
    for h in range(n_heads):
        k_col, q_col = kt[:, h:h + 1], qt[:, h:h + 1]
        sd = s_ref[h] * jnp.exp(gate[n_heads + h:n_heads + h + 1, :])
        pred = jnp.sum(sd * k_col, axis=0, keepdims=True)
        delta = (v[h:h + 1, :] - pred) * gate[h:h + 1, :]
        s1 = sd + k_col * delta
        so_ref[h] = s1
        outs.append(jnp.sum(s1 * q_col, axis=0, keepdims=True))
    o = jnp.concatenate(outs, axis=0)
    d_ref[...] = _gated_norm(o, g_ref[...], z_ref[...]).astype(d_ref.dtype)


def gdn_sample(x_new, state_conv, layer, conv_w, z, gate, gnorm_g, state_ssm):
    db, ch, _ = x_new.shape
    n_heads = ch // 3
    return pl.pallas_call(
        functools.partial(_gdn_step_kernel, n_heads=n_heads),
        grid=(db,),
        in_specs=[
            pl.BlockSpec((None, ch, DK), lambda b: (b, 0, 0)),
            pl.BlockSpec((None, None, CONV - 1, ch, DK), lambda b: (layer, b, 0, 0, 0)),
            pl.BlockSpec((CONV, ch, DK), lambda b: (0, 0, 0)),
            pl.BlockSpec((None, n_heads, DV), lambda b: (b, 0, 0)),
            pl.BlockSpec((None, 2 * n_heads, 1), lambda b: (b, 0, 0)),
            pl.BlockSpec((1, DV), lambda b: (0, 0)),
            pl.BlockSpec((None, None, n_heads, DK, DV), lambda b: (layer, b, 0, 0, 0)),
        ],
        out_specs=[
            pl.BlockSpec((None, n_heads, DV), lambda b: (b, 0, 0)),
            pl.BlockSpec((None, n_heads, DK, DV), lambda b: (b, 0, 0, 0)),
        ],
        out_shape=[
            jax.ShapeDtypeStruct((db, n_heads, DV), BF16),
            jax.ShapeDtypeStruct((db, n_heads, DK, DV), F32),
        ],
        compiler_params=_params("parallel"),
        name="gdn_sample",
    )(x_new, state_conv, conv_w, z, gate, gnorm_g.reshape(1, DV), state_ssm)


def kernel(x_prompt, x_sample, cache_k, cache_v, state_conv, state_ssm, page_table, c_prompt, c_sample, w_mod, b_mod, norm_mix_g, w_in, conv_w, a_log, dt_bias, lambda_q1, lambda_k1, lambda_q2, lambda_k2, subln_g, gdn_norm_g, w_out, norm_ffn_g, w_gate, w_up, w_down, final_norm_g):
    batch, seq, d = x_prompt.shape
    db, ds, _ = x_sample.shape
    assert ds == 1
    depth = w_in.shape[0]
    n_a = cache_k.shape[3]
    n_d = state_ssm.shape[2]
    conv_ch = state_conv.shape[3]
    past = page_table.shape[1] * cache_k.shape[2]
    m_p = batch * seq
    qkv_w = n_a * HD_QK
    col_conv = 3 * qkv_w
    col_z = col_conv + conv_ch
    col_ba = col_z + n_d * DV
    hg = math.gcd(n_d, GDN_HEADS_PER_STEP)

    c_all = jnp.zeros((MOD_ROWS, d), F32).at[:batch].set(c_prompt).at[SAMPLE_ROW0:SAMPLE_ROW0 + db].set(c_sample)
    mods = adaln(c_all, w_mod, b_mod)
    mods3 = mods.reshape(depth * MOD_ROWS * N_MOD, 1, d)
    mods_s = mods.reshape(depth, MOD_ROWS, N_MOD, d)[:, SAMPLE_ROW0:SAMPLE_ROW0 + db]

    tabs_p = _rope_tables(jnp.arange(seq), seq)
    tabs_s = _rope_tables(past + jnp.arange(ds), db)

    xp = x_prompt.reshape(m_p, d)
    xs = x_sample.reshape(db, d)
    outs = [[] for _ in range(8)]
    for l in range(depth):
        lam_init = 0.8 - 0.6 * math.exp(-0.3 * l)
        lam_vecs = jnp.stack([lambda_q1[l], lambda_k1[l], lambda_q2[l], lambda_k2[l]])
        w_in_b = cast_layer(w_in, l, col_ba)
        w_ba_b = w_in[l, :, col_ba:].astype(BF16)
        w_out_b = cast_layer(w_out, l)
        w_gate_b = cast_layer(w_gate, l)
        w_up_b = cast_layer(w_up, l)
        w_down_b = cast_layer(w_down, l)
        conv_w3 = conv_w[l].reshape(CONV, conv_ch // DK, DK)

        def gate_p(which):
            return lambda tm, tn: _gate_spec_prompt(tn, l, which, seq // tm)

        h = norm_mod_prompt(xp, norm_mix_g[l], mods3, l, 0, 1, seq)
        proj = matmul(h, w_in_b, col_ba)
        gcol, grow = gates(h, w_ba_b, a_log[l], dt_bias[l], GDN_CHUNK)
        qb, kf, kb, vf, vb = qkv_post(proj, tabs_p, n_a, seq)
        att = diff_attn_prompt(qb, kb, vb, lam_vecs, subln_g[l], lam_init, batch, seq, n_a)
        ngrp = n_d // hg
        gcol_g = jnp.concatenate([gcol[:, :n_d].reshape(m_p, ngrp, hg), gcol[:, n_d:].reshape(m_p, ngrp, hg)],
                                 axis=2).transpose(1, 0, 2)
        grow_g = jnp.concatenate([grow[:n_d].reshape(ngrp, hg, m_p), grow[n_d:].reshape(ngrp, hg, m_p)], axis=1)
        gdn_o, s_p = gdn_prompt(proj, col_conv, conv_w[l], gcol_g, grow_g, gdn_norm_g[l], batch, seq, n_d, hg=hg)
        xp = matmul2_residual(att, gdn_o, w_out_b, xp, gate_p(2), mods3, min(1024, seq), 512)
        outs[0].append(kf.reshape(batch, seq, n_a, HD_QK))
        outs[1].append(vf.reshape(batch, seq, n_a, HD_V))
        outs[2].append(proj.reshape(batch, seq, -1)[:, seq - (CONV - 1):, col_conv:col_z])
        outs[3].append(s_p)
        h = norm_mod_prompt(xp, norm_ffn_g[l], mods3, l, 3, 4, seq)
        act = gate_up(h, w_gate_b, w_up_b)
        xp = matmul_residual(act, w_down_b, xp, gate_p(5), mods3, min(512, seq), 256)

        ms = mods_s[l]

        def gate_s(which):
            g_rows = ms[:, which]
            return (lambda tm, tn: pl.BlockSpec((tm, tn), lambda i, j: (i, j))), g_rows

        h = norm_mod_sample(xs, norm_mix_g[l], ms[:, 0], ms[:, 1])
        proj = matmul(h, w_in_b, col_ba)
        gcol, _ = gates(h, w_ba_b, a_log[l], dt_bias[l], 0)
        qb, kf, _, vf, _ = qkv_post(proj, tabs_s, n_a, db)
        att = diff_attn_sample(qb.reshape(db, n_a, HD_QK), kf.reshape(db, n_a, HD_QK), vf.reshape(db, n_a, HD_V),
                               cache_k, cache_v, page_table, l, lam_vecs, subln_g[l], lam_init)
        cin = proj[:, col_conv:col_z]
        gdn_o, s_s = gdn_sample(cin.reshape(db, conv_ch // DK, DK),
                                state_conv.reshape(depth, db, CONV - 1, conv_ch // DK, DK), l, conv_w3,
                                proj[:, col_z:col_ba].reshape(db, n_d, DV), gcol.reshape(db, 2 * n_d, 1),
                                gdn_norm_g[l], state_ssm)
        spec, g_rows = gate_s(2)
        xs = matmul2_residual(att.reshape(db, n_a * HD_V), gdn_o.reshape(db, n_d * DV), w_out_b, xs, spec, g_rows,
                              db, 1024)
        outs[4].append(kf.reshape(db, ds, n_a, HD_QK))
        outs[5].append(vf.reshape(db, ds, n_a, HD_V))
        outs[6].append(jnp.concatenate([state_conv[l][:, 1:], cin[:, None, :]], axis=1))
        outs[7].append(s_s)
        h = norm_mod_sample(xs, norm_ffn_g[l], ms[:, 3], ms[:, 4])
        act = gate_up(h, w_gate_b, w_up_b, tn_want=256)
        spec, g_rows = gate_s(5)
        xs = matmul_residual(act, w_down_b, xs, spec, g_rows, db, 256)

    y_prompt = final_norm(xp, final_norm_g).reshape(batch, seq, d)
    y_sample = final_norm(xs, final_norm_g).reshape(db, ds, d)
    return (y_prompt, y_sample) + tuple(jnp.stack(o) for o in outs)
```

```python
import functools
import math

import jax
import jax.numpy as jnp
import numpy as np
from jax import lax
from jax.experimental import pallas as pl
from jax.experimental.pallas import tpu as pltpu

F32 = jnp.float32
BF16 = jnp.bfloat16

HD_V = 128
HD_C = 64
HD_QK = 2 * HD_C
DK = 128
DV = 128
ROT_DIM = HD_C // 4
ROT_HALF = ROT_DIM // 2
ROPE_THETA = 500000.0
Q_SCALE = HD_C ** -0.5 * math.log2(math.e)
CONV = 4
GDN_CHUNK = 64
N_MOD = 6
EPS = 1e-6
MOD_ROWS = 32
SAMPLE_ROW0 = 16
GDN_HEADS_PER_STEP = 8
GDN_ROWS_PER_STEP = 128
FLASH_SLAB = 64

VMEM_LIMIT_BYTES = 56 * 1024 * 1024
LANES = 128


def _params(*sem):
    return pltpu.CompilerParams(dimension_semantics=sem, vmem_limit_bytes=VMEM_LIMIT_BYTES)


def _tile(dim, want):
    if dim <= want:
        return dim
    t = want
    while dim % t:
        t -= 8
    return t


def _sigmoid(x):
    return 1.0 / (1.0 + jnp.exp(-x))


def _silu(x):
    return x * _sigmoid(x)


def _softplus(x):
    return jnp.maximum(x, 0.0) + jnp.log(1.0 + jnp.exp(-jnp.abs(x)))


def _adaln_kernel(c_ref, w_ref, b_ref, o_ref):
    acc = jnp.dot(c_ref[...].astype(BF16), w_ref[...].astype(BF16), preferred_element_type=F32)
    o_ref[...] = acc + b_ref[...]


def adaln(c_all, w_mod, b_mod):
    depth, d, n = w_mod.shape
    tn = _tile(n, 512)
    return pl.pallas_call(
        _adaln_kernel,
        grid=(depth, n // tn),
        in_specs=[
            pl.BlockSpec((MOD_ROWS, d), lambda l, j: (0, 0)),
            pl.BlockSpec((None, d, tn), lambda l, j: (l, 0, j)),
            pl.BlockSpec((None, 1, tn), lambda l, j: (l, 0, j)),
        ],
        out_specs=pl.BlockSpec((None, MOD_ROWS, tn), lambda l, j: (l, 0, j)),
        out_shape=jax.ShapeDtypeStruct((depth, MOD_ROWS, n), F32),
        compiler_params=_params("parallel", "parallel"),
        name="adaln",
    )(c_all, w_mod, b_mod.reshape(depth, 1, n))


def _norm_mod_kernel(x_ref, g_ref, shift_ref, scale_ref, o_ref):
    x = x_ref[...]
    y = x * lax.rsqrt(jnp.mean(x * x, axis=-1, keepdims=True) + EPS) * g_ref[...]
    o_ref[...] = (y * (1.0 + scale_ref[...]) + shift_ref[...]).astype(o_ref.dtype)


def _norm_kernel(x_ref, g_ref, o_ref):
    x = x_ref[...]
    o_ref[...] = (x * lax.rsqrt(jnp.mean(x * x, axis=-1, keepdims=True) + EPS) * g_ref[...]).astype(o_ref.dtype)


def norm_mod_prompt(x, g, mods3, layer, i_shift, i_scale, seq):
    m, d = x.shape
    tm = _tile(seq, 256)
    per = seq // tm

    def mod_idx(which):
        return lambda i: ((layer * MOD_ROWS + i // per) * N_MOD + which, 0, 0)

    return pl.pallas_call(
        _norm_mod_kernel,
        grid=(m // tm,),
        in_specs=[
            pl.BlockSpec((tm, d), lambda i: (i, 0)),
            pl.BlockSpec((1, d), lambda i: (0, 0)),
            pl.BlockSpec((None, 1, d), mod_idx(i_shift)),
            pl.BlockSpec((None, 1, d), mod_idx(i_scale)),
        ],
        out_specs=pl.BlockSpec((tm, d), lambda i: (i, 0)),
        out_shape=jax.ShapeDtypeStruct((m, d), BF16),
        compiler_params=_params("parallel"),
        name="norm_mod_prompt",
    )(x, g.reshape(1, d), mods3, mods3)


def norm_mod_sample(x, g, shift, scale):
    m, d = x.shape
    full = pl.BlockSpec((m, d), lambda: (0, 0))
    return pl.pallas_call(
        _norm_mod_kernel,
        in_specs=[full, pl.BlockSpec((1, d), lambda: (0, 0)), full, full],
        out_specs=full,
        out_shape=jax.ShapeDtypeStruct((m, d), BF16),
        name="norm_mod_sample",
    )(x, g.reshape(1, d), shift, scale)


def final_norm(x, g):
    m, d = x.shape
    tm = _tile(m, 256)
    return pl.pallas_call(
        _norm_kernel,
        grid=(m // tm,),
        in_specs=[pl.BlockSpec((tm, d), lambda i: (i, 0)), pl.BlockSpec((1, d), lambda i: (0, 0))],
        out_specs=pl.BlockSpec((tm, d), lambda i: (i, 0)),
        out_shape=jax.ShapeDtypeStruct((m, d), F32),
        compiler_params=_params("parallel"),
        name="final_norm",
    )(x, g.reshape(1, d))


def _lane_tile(n, want):
    best = n
    for t in range(LANES, min(n, want) + 1, LANES):
        if n % t == 0:
            best = t
    return best


def _cast_kernel(w_ref, o_ref):
    o_ref[...] = w_ref[...].astype(o_ref.dtype)


def cast_layer(w, layer, n_cols=None):
    _, k, n = w.shape
    n_cols = n if n_cols is None else n_cols
    tk, tn = _tile(k, 256), _lane_tile(n_cols, 8192)
    return pl.pallas_call(
        _cast_kernel,
        grid=(k // tk, n_cols // tn),
        in_specs=[pl.BlockSpec((None, tk, tn), lambda i, j: (layer, i, j))],
        out_specs=pl.BlockSpec((tk, tn), lambda i, j: (i, j)),
        out_shape=jax.ShapeDtypeStruct((k, n_cols), BF16),
        compiler_params=_params("parallel", "parallel"),
        name="cast_layer",
    )(w)


def _mm_kernel(a_ref, w_ref, o_ref):
    o_ref[...] = jnp.dot(a_ref[...], w_ref[...], preferred_element_type=F32).astype(o_ref.dtype)


def matmul(a, w, n_out, tm_want=1024, tn_want=1024):
    m, k = a.shape
    tm, tn = _tile(m, tm_want), _tile(n_out, tn_want)
    return pl.pallas_call(
        _mm_kernel,
        grid=(m // tm, n_out // tn),
        in_specs=[pl.BlockSpec((tm, k), lambda i, j: (i, 0)), pl.BlockSpec((k, tn), lambda i, j: (0, j))],
        out_specs=pl.BlockSpec((tm, tn), lambda i, j: (i, j)),
        out_shape=jax.ShapeDtypeStruct((m, n_out), F32),
        compiler_params=_params("parallel", "parallel"),
        name="matmul",
    )(a, w)


def _mm_res_kernel(a_ref, w_ref, x_ref, gate_ref, o_ref):
    acc = jnp.dot(a_ref[...], w_ref[...], preferred_element_type=F32)
    o_ref[...] = x_ref[...] + gate_ref[...] * acc


def _mm2_res_kernel(a_ref, d_ref, wa_ref, wd_ref, x_ref, gate_ref, o_ref):
    acc = jnp.dot(a_ref[...], wa_ref[...], preferred_element_type=F32)
    acc = acc + jnp.dot(d_ref[...], wd_ref[...], preferred_element_type=F32)
    o_ref[...] = x_ref[...] + gate_ref[...] * acc


def _gate_spec_prompt(tn, layer, which, per):
    return pl.BlockSpec((None, 1, tn), lambda i, j: ((layer * MOD_ROWS + i // per) * N_MOD + which, 0, j))


def matmul_residual(a, w, x, gate_spec, gate, tm_want, tn_want):
    m, k = a.shape
    n = w.shape[1]
    tm, tn = _tile(m, tm_want), _tile(n, tn_want)
    return pl.pallas_call(
        _mm_res_kernel,
        grid=(m // tm, n // tn),
        in_specs=[
            pl.BlockSpec((tm, k), lambda i, j: (i, 0)),
            pl.BlockSpec((k, tn), lambda i, j: (0, j)),
            pl.BlockSpec((tm, tn), lambda i, j: (i, j)),
            gate_spec(tm, tn),
        ],
        out_specs=pl.BlockSpec((tm, tn), lambda i, j: (i, j)),
        out_shape=jax.ShapeDtypeStruct((m, n), F32),
        compiler_params=_params("parallel", "parallel"),
        name="matmul_residual",
    )(a, w, x, gate)


def matmul2_residual(a, d, w, x, gate_spec, gate, tm_want, tn_want):
    m, kh = a.shape
    n = w.shape[1]
    tm, tn = _tile(m, tm_want), _tile(n, tn_want)
    return pl.pallas_call(
        _mm2_res_kernel,
        grid=(m // tm, n // tn),
        in_specs=[
            pl.BlockSpec((tm, kh), lambda i, j: (i, 0)),
            pl.BlockSpec((tm, kh), lambda i, j: (i, 0)),
            pl.BlockSpec((kh, tn), lambda i, j: (0, j)),
            pl.BlockSpec((kh, tn), lambda i, j: (1, j)),
            pl.BlockSpec((tm, tn), lambda i, j: (i, j)),
            gate_spec(tm, tn),
        ],
        out_specs=pl.BlockSpec((tm, tn), lambda i, j: (i, j)),
        out_shape=jax.ShapeDtypeStruct((m, n), F32),
        compiler_params=_params("parallel", "parallel"),
        name="matmul2_residual",
    )(a, d, w, w, x, gate)


def _gateup_kernel(h_ref, wg_ref, wu_ref, o_ref):
    h = h_ref[...]
    g = jnp.dot(h, wg_ref[...], preferred_element_type=F32)
    u = jnp.dot(h, wu_ref[...], preferred_element_type=F32)
    o_ref[...] = (_silu(g) * u).astype(o_ref.dtype)


def gate_up(h, wg, wu, tm_want=1024, tn_want=256):
    m, k = h.shape
    n = wg.shape[1]
    tm, tn = _tile(m, tm_want), _tile(n, tn_want)
    return pl.pallas_call(
        _gateup_kernel,
        grid=(m // tm, n // tn),
        in_specs=[
            pl.BlockSpec((tm, k), lambda i, j: (i, 0)),
            pl.BlockSpec((k, tn), lambda i, j: (0, j)),
            pl.BlockSpec((k, tn), lambda i, j: (0, j)),
        ],
        out_specs=pl.BlockSpec((tm, tn), lambda i, j: (i, j)),
        out_shape=jax.ShapeDtypeStruct((m, n), BF16),
        compiler_params=_params("parallel", "parallel"),
        name="gate_up",
    )(h, wg, wu)


def _first_row_block():
    return pl.program_id(1) == 0


def _wres_mm_kernel(a_ref, as_ref, wt_ref, o_ref, os_ref, w_scr):
    @pl.when(_first_row_block())
    def _():
        w_scr[...] = wt_ref[...].T.astype(BF16)
        os_ref[...] = jnp.dot(as_ref[...], w_scr[...], preferred_element_type=F32)

    o_ref[...] = jnp.dot(a_ref[...], w_scr[...], preferred_element_type=F32)


def matmul_both(a, a_s, w_t, layer, n_out, seq, tm_want=1024, tn_want=512):
    m, k = a.shape
    ms = a_s.shape[0]
    tm, tn = _tile(m, min(tm_want, seq)), _lane_tile(n_out, tn_want)
    return pl.pallas_call(
        _wres_mm_kernel,
        grid=(n_out // tn, m // tm),
        in_specs=[
            pl.BlockSpec((tm, k), lambda j, i: (i, 0)),
            pl.BlockSpec((ms, k), lambda j, i: (0, 0)),
            pl.BlockSpec((None, tn, k), lambda j, i: (layer, j, 0)),
        ],
        out_specs=[pl.BlockSpec((tm, tn), lambda j, i: (i, j)), pl.BlockSpec((ms, tn), lambda j, i: (0, j))],
        out_shape=[jax.ShapeDtypeStruct((m, n_out), F32), jax.ShapeDtypeStruct((ms, n_out), F32)],
        scratch_shapes=[pltpu.VMEM((k, tn), BF16)],
        compiler_params=_params("arbitrary", "arbitrary"),
        name="matmul_both",
    )(a, a_s, w_t)


def _wres_gateup_kernel(h_ref, hs_ref, wg_ref, wu_ref, o_ref, os_ref, wg_scr, wu_scr):
    @pl.when(_first_row_block())
    def _():
        wg_scr[...] = wg_ref[...].astype(BF16)
        wu_scr[...] = wu_ref[...].astype(BF16)
        hs = hs_ref[...]
        gs = jnp.dot(hs, wg_scr[...], preferred_element_type=F32)
        us = jnp.dot(hs, wu_scr[...], preferred_element_type=F32)
        os_ref[...] = (_silu(gs) * us).astype(os_ref.dtype)

    h = h_ref[...]
    g = jnp.dot(h, wg_scr[...], preferred_element_type=F32)
    u = jnp.dot(h, wu_scr[...], preferred_element_type=F32)
    o_ref[...] = (_silu(g) * u).astype(o_ref.dtype)


def gate_up_both(h, h_s, wg, wu, layer, seq, tm_want=1024, tn_want=256):
    m, k = h.shape
    ms = h_s.shape[0]
    n = wg.shape[2]
    tm, tn = _tile(m, min(tm_want, seq)), _lane_tile(n, tn_want)
    wspec = pl.BlockSpec((None, k, tn), lambda j, i: (layer, 0, j))
    return pl.pallas_call(
        _wres_gateup_kernel,
        grid=(n // tn, m // tm),
        in_specs=[pl.BlockSpec((tm, k), lambda j, i: (i, 0)), pl.BlockSpec((ms, k), lambda j, i: (0, 0)), wspec, wspec],
        out_specs=[pl.BlockSpec((tm, tn), lambda j, i: (i, j)), pl.BlockSpec((ms, tn), lambda j, i: (0, j))],
        out_shape=[jax.ShapeDtypeStruct((m, n), BF16), jax.ShapeDtypeStruct((ms, n), BF16)],
        scratch_shapes=[pltpu.VMEM((k, tn), BF16)] * 2,
        compiler_params=_params("arbitrary", "arbitrary"),
        name="gate_up_both",
    )(h, h_s, wg, wu)


def _wres_out_kernel(a_ref, d_ref, as_ref, ds_ref, wa_ref, wd_ref, x_ref, gate_ref, xs_ref, gates_ref,
                     o_ref, os_ref, wa_scr, wd_scr):
    @pl.when(_first_row_block())
    def _():
        wa_scr[...] = wa_ref[...].astype(BF16)
        wd_scr[...] = wd_ref[...].astype(BF16)
        acc_s = jnp.dot(as_ref[...], wa_scr[...], preferred_element_type=F32)
        acc_s = acc_s + jnp.dot(ds_ref[...], wd_scr[...], preferred_element_type=F32)
        os_ref[...] = xs_ref[...] + gates_ref[...] * acc_s

    acc = jnp.dot(a_ref[...], wa_scr[...], preferred_element_type=F32)
    acc = acc + jnp.dot(d_ref[...], wd_scr[...], preferred_element_type=F32)
    o_ref[...] = x_ref[...] + gate_ref[...] * acc


def out_proj_both(a, d, a_s, d_s, w, layer, x, mods3, which, x_s, gate_s, seq, tm_want=1024, tn_want=512):
    m, kh = a.shape
    ms = a_s.shape[0]
    n = w.shape[2]
    tm, tn = _tile(m, min(tm_want, seq)), _lane_tile(n, tn_want)
    per = seq // tm
    half = lambda r: pl.BlockSpec((None, kh, tn), lambda j, i, r=r: (layer, r, j))
    rows = pl.BlockSpec((tm, kh), lambda j, i: (i, 0))
    rows_s = pl.BlockSpec((ms, kh), lambda j, i: (0, 0))
    tile_s = pl.BlockSpec((ms, tn), lambda j, i: (0, j))
    return pl.pallas_call(
        _wres_out_kernel,
        grid=(n // tn, m // tm),
        in_specs=[
            rows, rows, rows_s, rows_s, half(0), half(1),
            pl.BlockSpec((tm, tn), lambda j, i: (i, j)),
            pl.BlockSpec((None, 1, tn), lambda j, i: ((layer * MOD_ROWS + i // per) * N_MOD + which, 0, j)),
            tile_s, tile_s,
        ],
        out_specs=[pl.BlockSpec((tm, tn), lambda j, i: (i, j)), tile_s],
        out_shape=[jax.ShapeDtypeStruct((m, n), F32), jax.ShapeDtypeStruct((ms, n), F32)],
        scratch_shapes=[pltpu.VMEM((kh, tn), BF16)] * 2,
        compiler_params=_params("arbitrary", "arbitrary"),
        name="out_proj_both",
    )(a, d, a_s, d_s, w, w, x, mods3, x_s, gate_s)


def _gates_kernel(h_ref, w_ref, pcol_ref, prow_ref, col_ref, row_ref, *, n_heads, chunk):
    wrow = lax.broadcasted_iota(jnp.int32, w_ref.shape, 0)
    w = jnp.where(wrow < 2 * n_heads, w_ref[...], 0.0).astype(BF16)
    wide = lax.dot_general(h_ref[...], w, (((1,), (1,)), ((), ())), preferred_element_type=F32)
    col = wide[:, :2 * n_heads]
    row = wide.T[:2 * n_heads, :]

    def gate(x, is_beta, a_log, dt_bias):
        return jnp.where(is_beta, _sigmoid(x), -jnp.exp(a_log) * _softplus(x + dt_bias))

    lane = lax.broadcasted_iota(jnp.int32, col.shape, 1)
    col = gate(col, lane < n_heads, pcol_ref[0:1, :], pcol_ref[1:2, :])
    sub = lax.broadcasted_iota(jnp.int32, row.shape, 0)
    row = gate(row, sub < n_heads, prow_ref[:, 0:1], prow_ref[:, 1:2])
    if chunk:
        t_col = lax.broadcasted_iota(jnp.int32, col.shape, 0) % chunk
        t_row = lax.broadcasted_iota(jnp.int32, row.shape, 1) % chunk
        s = 1
        while s < chunk:
            col = col + jnp.where((t_col >= s) & (lane >= n_heads), pltpu.roll(col, s, axis=0), 0.0)
            row = row + jnp.where((t_row >= s) & (sub >= n_heads), pltpu.roll(row, s, axis=1), 0.0)
            s *= 2
    col_ref[...] = col
    row_ref[...] = row


def gates(h, w_t, layer, col_ba, a_log, dt_bias, chunk):
    m, k = h.shape
    n2 = w_t.shape[1] - col_ba
    nh = n2 // 2
    assert col_ba % LANES == 0 and n2 <= LANES
    tm = _tile(m, 512)
    zeros = jnp.zeros((nh,), F32)
    pcol = jnp.stack([jnp.concatenate([zeros, a_log]), jnp.concatenate([zeros, dt_bias])])
    return pl.pallas_call(
        functools.partial(_gates_kernel, n_heads=nh, chunk=chunk),
        grid=(m // tm,),
        in_specs=[
            pl.BlockSpec((tm, k), lambda i: (i, 0)),
            pl.BlockSpec((None, LANES, k), lambda i: (layer, col_ba // LANES, 0)),
            pl.BlockSpec((2, n2), lambda i: (0, 0)),
            pl.BlockSpec((n2, 2), lambda i: (0, 0)),
        ],
        out_specs=[pl.BlockSpec((tm, n2), lambda i: (i, 0)), pl.BlockSpec((n2, tm), lambda i: (0, i))],
        out_shape=[jax.ShapeDtypeStruct((m, n2), F32), jax.ShapeDtypeStruct((n2, m), F32)],
        compiler_params=_params("parallel"),
        name="gates",
    )(h, w_t, pcol, pcol.T)


def _rope_tables(pos, rows):
    inv = ROPE_THETA ** (-jnp.arange(ROT_HALF, dtype=F32) * (2.0 / ROT_DIM))
    ang = pos.astype(F32)[:, None] * inv[None, :]
    cos, sin = jnp.cos(ang), jnp.sin(ang)
    n = pos.shape[0]
    one = jnp.ones((n, HD_C - ROT_DIM), F32)
    zero = jnp.zeros((n, HD_C - ROT_DIM), F32)
    zh = jnp.zeros((n, ROT_HALF), F32)
    c = jnp.concatenate([cos, cos, one], axis=1)
    sa = jnp.concatenate([-sin, zh, zero], axis=1)
    sb = jnp.concatenate([zh, sin, zero], axis=1)
    tabs = [jnp.concatenate([t, t], axis=1) for t in (c, sa, sb)]
    if n != rows:
        tabs = [jnp.broadcast_to(t, (rows, 2 * HD_C)) for t in tabs]
    return tabs


def _qkv_post_kernel(q_ref, k_ref, v_ref, c_ref, sa_ref, sb_ref, qb_ref, kf_ref, kb_ref, vf_ref, vb_ref):
    c, sa, sb = c_ref[...], sa_ref[...], sb_ref[...]

    def rope(x):
        return x * c + pltpu.roll(x, LANES - ROT_HALF, axis=1) * sa + pltpu.roll(x, ROT_HALF, axis=1) * sb

    q = rope(q_ref[...])
    k = rope(k_ref[...])
    qb_ref[...] = (q * Q_SCALE).astype(BF16)
    kf_ref[...] = k
    kb_ref[...] = k.astype(BF16)
    v = v_ref[...]
    vf_ref[...] = v
    vb_ref[...] = v.astype(BF16)


def qkv_post(proj, tabs, n_heads, rows_per_seq):
    m = proj.shape[0]
    tm = _tile(rows_per_seq, 1024)
    per = rows_per_seq // tm
    width = n_heads * HD_QK
    blk = lambda off: pl.BlockSpec((tm, HD_QK), lambda i, h, off=off: (i, off + h))
    tab = pl.BlockSpec((tm, HD_QK), lambda i, h: (i % per, 0))
    out = pl.BlockSpec((tm, HD_QK), lambda i, h: (i, h))
    return pl.pallas_call(
        _qkv_post_kernel,
        grid=(m // tm, n_heads),
        in_specs=[blk(0), blk(n_heads), blk(2 * n_heads), tab, tab, tab],
        out_specs=[out] * 5,
        out_shape=[jax.ShapeDtypeStruct((m, width), dt) for dt in (BF16, F32, BF16, F32, BF16)],
        compiler_params=_params("parallel", "parallel"),
        name="qkv_post",
    )(proj, proj, proj, *tabs)


def _lambda_value(lam_ref, lam_init):
    l1 = jnp.sum(lam_ref[0:1, :] * lam_ref[1:2, :], axis=1, keepdims=True)
    l2 = jnp.sum(lam_ref[2:3, :] * lam_ref[3:4, :], axis=1, keepdims=True)
    return jnp.exp(l1) - jnp.exp(l2) + lam_init


def _subln(o, g, lam_init):
    y = o * lax.rsqrt(jnp.mean(o * o, axis=-1, keepdims=True) + EPS) * g
    return y * (1.0 - lam_init)


def _split_maps(q):
    lane = lax.broadcasted_iota(jnp.int32, q.shape, 1)
    zero = jnp.zeros_like(q)
    return jnp.concatenate([jnp.where(lane < HD_C, q, zero), jnp.where(lane >= HD_C, q, zero)], axis=0)


def _flash_kernel(q_ref, k_ref, v_ref, lam_ref, g_ref, o_ref, *, seq, tq, tk, hq, lam_init):
    lam = _lambda_value(lam_ref, lam_init)
    gain = g_ref[...]
    lanes = [slice(h * HD_QK, (h + 1) * HD_QK) for h in range(hq)]
    nt_dims = (((1,), (1,)), ((), ()))
    tn_dims = (((0,), (0,)), ((), ()))
    key_off = lax.broadcasted_iota(jnp.int32, (tk, 2 * tq), 0)
    qry_off = lax.broadcasted_iota(jnp.int32, (tk, 2 * tq), 1) % tq

    def scores(kb, q2):
        k0 = pl.multiple_of(kb * tk, tk)
        return tuple(lax.dot_general(k_ref[pl.ds(k0, tk), ln], qq, nt_dims, preferred_element_type=F32)
                     for ln, qq in zip(lanes, q2))

    def softmax_pv(kb, m, l, acc, s):
        k0 = pl.multiple_of(kb * tk, tk)
        m_out, l_out, acc_out = [], [], []
        for ln, mh, lh, ah, sh in zip(lanes, m, l, acc, s):
            m_new = jnp.maximum(mh, jnp.max(sh, axis=0, keepdims=True))
            alpha = jnp.exp2(mh - m_new)
            row_sum, slabs = None, []
            for r0 in range(0, tk, FLASH_SLAB):
                pj = jnp.exp2(sh[r0:r0 + FLASH_SLAB] - m_new)
                sj = jnp.sum(pj, axis=0, keepdims=True)
                row_sum = sj if row_sum is None else row_sum + sj
                slabs.append(pj.astype(BF16))
            pv = lax.dot_general(v_ref[pl.ds(k0, tk), ln], jnp.concatenate(slabs, axis=0), tn_dims,
                                 preferred_element_type=F32)
            m_out.append(m_new)
            l_out.append(alpha * lh + row_sum)
            acc_out.append(alpha * ah + pv)
        return tuple(m_out), tuple(l_out), tuple(acc_out)

    def q_block(qb, _):
        q0 = pl.multiple_of(qb * tq, tq)
        q2 = [_split_maps(q_ref[pl.ds(q0, tq), ln]) for ln in lanes]
        n_full = q0 // tk

        def full_step(kb, carry):
            m, l, acc, s = carry
            s_next = scores(kb + 1, q2)
            return softmax_pv(kb, m, l, acc, s) + (s_next,)

        init = (tuple(jnp.full((1, 2 * tq), -jnp.inf, F32) for _ in lanes),
                tuple(jnp.zeros((1, 2 * tq), F32) for _ in lanes),
                tuple(jnp.zeros((HD_V, 2 * tq), F32) for _ in lanes),
                scores(0, q2))
        m, l, acc, s = lax.fori_loop(0, n_full, full_step, init)
        keep = key_off + n_full * tk <= qry_off + q0
        s = [jnp.where(keep, a, -jnp.inf) for a in s]
        _, l, acc = softmax_pv(n_full, m, l, acc, s)
        for ln, lh, ah in zip(lanes, l, acc):
            o2 = ah / lh
            o = (o2[:, :tq] - lam * o2[:, tq:]).T
            o_ref[pl.ds(q0, tq), ln] = _subln(o, gain, lam_init).astype(o_ref.dtype)
        return 0

    lax.fori_loop(0, seq // tq, q_block, 0)


def diff_attn_prompt(qb, kb, vb, lam_vecs, subln_g, lam_init, batch, seq, n_heads, tq=256, tk=256, hq=2):
    tk = _tile(seq, tk)
    tq = _tile(tk, tq)
    hq = math.gcd(n_heads, hq)
    assert tk % tq == 0 and tq % LANES == 0
    blk = pl.BlockSpec((seq, hq * HD_QK), lambda b, g: (b, g))
    return pl.pallas_call(
        functools.partial(_flash_kernel, seq=seq, tq=tq, tk=tk, hq=hq, lam_init=lam_init),
        grid=(batch, n_heads // hq),
        in_specs=[blk, blk, blk,
                  pl.BlockSpec((4, HD_C), lambda b, g: (0, 0)),
                  pl.BlockSpec((1, HD_V), lambda b, g: (0, 0))],
        out_specs=blk,
        out_shape=jax.ShapeDtypeStruct((batch * seq, n_heads * HD_V), BF16),
        compiler_params=_params("parallel", "parallel"),
        name="diff_attn_prompt",
    )(qb, kb, vb, lam_vecs, subln_g.reshape(1, HD_V))


def _paged_kernel(pt_ref, q_ref, kn_ref, vn_ref, lam_ref, g_ref, *rest, n_heads, pages, lam_init):
    k_refs, v_refs = rest[:pages], rest[pages:2 * pages]
    o_ref, q2_scr, m_scr, l_scr, acc_scr = rest[2 * pages:]
    pg = pl.program_id(1)

    @pl.when(pg == 0)
    def _():
        q2 = _split_maps(q_ref[...])
        q2_scr[...] = q2
        kn = kn_ref[...].astype(BF16).astype(F32)
        vn = vn_ref[...].astype(BF16).astype(F32)
        s_new = jnp.sum(q2.astype(F32) * jnp.concatenate([kn, kn], axis=0), axis=1, keepdims=True)
        m_scr[...] = s_new
        l_scr[...] = jnp.ones(l_scr.shape, F32)
        acc_scr[...] = jnp.concatenate([vn, vn], axis=0)

    q2 = q2_scr[...]
    for kp_ref, vp_ref in zip(k_refs, v_refs):
        kp = kp_ref[...].reshape(-1, HD_QK).astype(BF16)
        vp = vp_ref[...].reshape(-1, HD_V).astype(BF16)
        s = lax.dot_general(q2, kp, (((1,), (1,)), ((), ())), preferred_element_type=F32)
        row_h = lax.broadcasted_iota(jnp.int32, s.shape, 0) % n_heads
        col_h = lax.broadcasted_iota(jnp.int32, s.shape, 1) % n_heads
        s = jnp.where(row_h == col_h, s, -jnp.inf)
        m_prev = m_scr[...]
        m_next = jnp.maximum(m_prev, jnp.max(s, axis=1, keepdims=True))
        alpha = jnp.exp2(m_prev - m_next)
        p = jnp.exp2(s - m_next)
        l_scr[...] = alpha * l_scr[...] + jnp.sum(p, axis=1, keepdims=True)
        acc_scr[...] = alpha * acc_scr[...] + jnp.dot(p.astype(BF16), vp, preferred_element_type=F32)
        m_scr[...] = m_next

    @pl.when(pg == pl.num_programs(1) - 1)
    def _():
        o2 = acc_scr[...] / l_scr[...]
        lam = _lambda_value(lam_ref, lam_init)
        o = o2[:n_heads] - lam * o2[n_heads:]
        o_ref[...] = _subln(o, g_ref[...], lam_init).astype(o_ref.dtype)


def diff_attn_sample(qb, k_new, v_new, cache_k, cache_v, page_table, layer, lam_vecs, subln_g, lam_init,
                     pages_per_step=8):
    db, n_heads, _ = qb.shape
    n_pages = page_table.shape[1]
    page = cache_k.shape[2]
    pps = pages_per_step
    while n_pages % pps:
        pps -= 1

    def page_spec(i):
        return pl.BlockSpec((None, None, page, n_heads, HD_QK),
                            lambda b, pg, pt, i=i: (layer, pt[b, pg * pps + i], 0, 0, 0))

    row = lambda b, pg, pt: (b, 0, 0)
    grid_spec = pltpu.PrefetchScalarGridSpec(
        num_scalar_prefetch=1,
        grid=(db, n_pages // pps),
        in_specs=[
            pl.BlockSpec((None, n_heads, HD_QK), row),
            pl.BlockSpec((None, n_heads, HD_QK), row),
            pl.BlockSpec((None, n_heads, HD_V), row),
            pl.BlockSpec((4, HD_C), lambda b, pg, pt: (0, 0)),
            pl.BlockSpec((1, HD_V), lambda b, pg, pt: (0, 0)),
        ] + [page_spec(i) for i in range(pps)] * 2,
        out_specs=pl.BlockSpec((None, n_heads, HD_V), row),
        scratch_shapes=[
            pltpu.VMEM((2 * n_heads, HD_QK), BF16),
            pltpu.VMEM((2 * n_heads, 1), F32),
            pltpu.VMEM((2 * n_heads, 1), F32),
            pltpu.VMEM((2 * n_heads, HD_V), F32),
        ],
    )
    return pl.pallas_call(
        functools.partial(_paged_kernel, n_heads=n_heads, pages=pps, lam_init=lam_init),
        grid_spec=grid_spec,
        out_shape=jax.ShapeDtypeStruct((db, n_heads, HD_V), BF16),
        compiler_params=_params("parallel", "arbitrary"),
        name="diff_attn_sample",
    )(page_table, qb, k_new, v_new, lam_vecs, subln_g.reshape(1, HD_V),
      *([cache_k] * pps), *([cache_v] * pps))


def _l2n(x):
    return x * lax.rsqrt(jnp.sum(x * x, axis=-1, keepdims=True) + EPS)


def _gated_norm(o, g, z):
    return o * lax.rsqrt(jnp.mean(o * o, axis=-1, keepdims=True) + EPS) * g * _silu(z)


def _dot_b(a, b):
    return jnp.dot(a.astype(BF16), b.astype(BF16), preferred_element_type=F32)


def _unit_lower_inverses(lows):
    c = lows[0].shape[0]
    i = lax.broadcasted_iota(jnp.int32, (c, c), 0)
    j = lax.broadcasted_iota(jnp.int32, (c, c), 1)
    eye = jnp.where(i == j, 1.0, 0.0).astype(F32)
    same16 = (i // 16) == (j // 16)
    same32 = (i // 32) == (j // 32)
    off16 = same32 & jnp.logical_not(same16)
    d = [jnp.where(same16, low, 0.0) for low in lows]
    d2 = [_dot_b(a, a) for a in d]
    d4 = [_dot_b(a, a) for a in d2]
    d8 = [_dot_b(a, a) for a in d4]
    x = [eye - a for a in d]
    x = [a + _dot_b(a, p) for a, p in zip(x, d2)]
    x = [a + _dot_b(a, p) for a, p in zip(x, d4)]
    x = [a + _dot_b(a, p) for a, p in zip(x, d8)]
    t = [_dot_b(jnp.where(off16, low, 0.0), a) for low, a in zip(lows, x)]
    x = [a - _dot_b(a, p) for a, p in zip(x, t)]
    t = [_dot_b(jnp.where(same32, 0.0, low), a) for low, a in zip(lows, x)]
    x = [a - _dot_b(a, p) for a, p in zip(x, t)]
    return x


def _causal_conv_block(x, halo, w):
    row8 = lax.broadcasted_iota(jnp.int32, halo.shape, 0)
    out = None
    for jtap in range(CONV):
        s = CONV - 1 - jtap
        if s == 0:
            xs = x
        else:
            rolled = pltpu.roll(x, s, axis=0)
            head = jnp.where(row8 < s, pltpu.roll(halo, s, axis=0), rolled[:8])
            xs = jnp.concatenate([head, rolled[8:]], axis=0)
        term = xs * w[jtap:jtap + 1, :]
        out = term if out is None else out + term
    return out


def _gdn_chunk_kernel(q_ref, k_ref, v_ref, z_ref, wq_ref, wk_ref, wv_ref, gcol_ref, grow_ref, g_ref,
                      d_ref, s_ref, hq_scr, hk_scr, hv_scr, *, hg, tb):
    ti = pl.program_id(2)
    c = GDN_CHUNK

    @pl.when(ti == 0)
    def _():
        hq_scr[...] = jnp.zeros(hq_scr.shape, F32)
        hk_scr[...] = jnp.zeros(hk_scr.shape, F32)
        hv_scr[...] = jnp.zeros(hv_scr.shape, F32)
        s_ref[...] = jnp.zeros(s_ref.shape, F32)

    xq, xk, xv = q_ref[...], k_ref[...], v_ref[...]
    aq = _silu(_causal_conv_block(xq, hq_scr[...], wq_ref[...]))
    ak = _silu(_causal_conv_block(xk, hk_scr[...], wk_ref[...]))
    av = _silu(_causal_conv_block(xv, hv_scr[...], wv_ref[...]))
    hq_scr[...] = xq[tb - 8:]
    hk_scr[...] = xk[tb - 8:]
    hv_scr[...] = xv[tb - 8:]

    ii = lax.broadcasted_iota(jnp.int32, (c, c), 0)
    jj = lax.broadcasted_iota(jnp.int32, (c, c), 1)
    gcol = gcol_ref[...]
    grow = grow_ref[...]
    gnorm = g_ref[...]

    nc = tb // c
    items = [(h, ci) for ci in range(nc) for h in range(hg)]
    lanes = lambda h: slice(h * DK, (h + 1) * DK)
    rows = lambda ci: slice(ci * c, (ci + 1) * c)
    nt_dims = (((1,), (1,)), ((), ()))
    tn_dims = (((0,), (0,)), ((), ()))

    q = [_l2n(aq[rows(ci), lanes(h)]) * (DK ** -0.5) for h, ci in items]
    k = [_l2n(ak[rows(ci), lanes(h)]) for h, ci in items]
    v = [av[rows(ci), lanes(h)] for h, ci in items]
    beta = [gcol[rows(ci), h:h + 1] for h, ci in items]
    gc_col = [gcol[rows(ci), hg + h:hg + h + 1] for h, ci in items]
    gc_row = [grow[hg + h:hg + h + 1, rows(ci)] for h, ci in items]
    decay = [jnp.exp(jnp.where(ii >= jj, a - b, -jnp.inf)) for a, b in zip(gc_col, gc_row)]
    kb = [a * b for a, b in zip(k, beta)]
    both = [lax.dot_general(jnp.concatenate([a, b], axis=0).astype(BF16), kk.astype(BF16), nt_dims,
                            preferred_element_type=F32) for a, b, kk in zip(kb, q, k)]
    low = [jnp.where(ii > jj, a[:c] * dc, 0.0) for a, dc in zip(both, decay)]
    a_in = [(a[c:] * dc).astype(BF16) for a, dc in zip(both, decay)]
    tmat = _unit_lower_inverses(low)
    eg = [jnp.exp(a) for a in gc_col]
    rhs = [jnp.concatenate([vv * b, kk * e], axis=1) for vv, b, kk, e in zip(v, beta, kb, eg)]
    uw = [_dot_b(t, r) for t, r in zip(tmat, rhs)]
    g_last = [a[c - 1:c, :] for a in gc_col]
    wq = [jnp.concatenate([a[:, DV:], qq * e], axis=0).astype(BF16) for a, qq, e in zip(uw, q, eg)]
    kt = [(kk * jnp.exp(gl - a)).astype(BF16) for kk, gl, a in zip(k, g_last, gc_col)]
    e_last = [jnp.exp(gl) for gl in g_last]

    state = [s_ref[h] for h in range(hg)]
    for ci in range(nc):
        sl = slice(ci * hg, (ci + 1) * hg)
        ws_qs = [jnp.dot(a, s.astype(BF16), preferred_element_type=F32) for a, s in zip(wq[sl], state)]
        v_new = [(a[:, :DV] - b[:c]).astype(BF16) for a, b in zip(uw[sl], ws_qs)]
        o = [b[c:] + jnp.dot(a, vn, preferred_element_type=F32) for b, a, vn in zip(ws_qs, a_in[sl], v_new)]
        state = [s * e + lax.dot_general(a, vn, tn_dims, preferred_element_type=F32)
                 for s, e, a, vn in zip(state, e_last[sl], kt[sl], v_new)]
        for h in range(hg):
            d_ref[rows(ci), lanes(h)] = _gated_norm(o[h], gnorm, z_ref[rows(ci), lanes(h)]).astype(d_ref.dtype)
    for h in range(hg):
        s_ref[h] = state[h]


def gdn_prompt(proj, col0, conv_w, gcol, grow, gnorm_g, batch, seq, n_heads, hg, tb=GDN_ROWS_PER_STEP):
    tb = _tile(seq, tb)
    nt = seq // tb
    wblk = hg * DK
    ngrp = n_heads // hg
    assert col0 % wblk == 0
    c0 = col0 // wblk

    def xin(off):
        return pl.BlockSpec((tb, wblk), lambda b, g, t, off=off: (b * nt + t, c0 + off * ngrp + g))

    def win(off):
        return pl.BlockSpec((CONV, wblk), lambda b, g, t, off=off: (0, off * ngrp + g))

    return pl.pallas_call(
        functools.partial(_gdn_chunk_kernel, hg=hg, tb=tb),
        grid=(batch, ngrp, nt),
        in_specs=[
            xin(0), xin(1), xin(2), xin(3),
            win(0), win(1), win(2),
            pl.BlockSpec((None, tb, 2 * hg), lambda b, g, t: (g, b * nt + t, 0)),
            pl.BlockSpec((None, 2 * hg, tb), lambda b, g, t: (g, 0, b * nt + t)),
            pl.BlockSpec((1, DV), lambda b, g, t: (0, 0)),
        ],
        out_specs=[
            pl.BlockSpec((tb, wblk), lambda b, g, t: (b * nt + t, g)),
            pl.BlockSpec((None, hg, DK, DV), lambda b, g, t: (b, g, 0, 0)),
        ],
        out_shape=[
            jax.ShapeDtypeStruct((batch * seq, n_heads * DV), BF16),
            jax.ShapeDtypeStruct((batch, n_heads, DK, DV), F32),
        ],
        scratch_shapes=[pltpu.VMEM((8, wblk), F32)] * 3,
        compiler_params=_params("parallel", "parallel", "arbitrary"),
        name="gdn_prompt",
    )(proj, proj, proj, proj, conv_w, conv_w, conv_w, gcol, grow, gnorm_g.reshape(1, DV))


def _gdn_step_kernel(x_ref, st_ref, w_ref, z_ref, gate_ref, g_ref, s_ref, d_ref, so_ref, *, n_heads):
    w = w_ref[...]
    conv = st_ref[0] * w[0]
    for jtap in range(1, CONV - 1):
        conv = conv + st_ref[jtap] * w[jtap]
    conv = conv + x_ref[...] * w[CONV - 1]
    act = _silu(conv)
    q = _l2n(act[:n_heads]) * (DK ** -0.5)
    k = _l2n(act[n_heads:2 * n_heads])
    v = act[2 * n_heads:]
    qt, kt = q.T, k.T
    gate = gate_ref[...]
    outs = []
    for h in range(n_heads):
        k_col, q_col = kt[:, h:h + 1], qt[:, h:h + 1]
        sd = s_ref[h] * jnp.exp(gate[n_heads + h:n_heads + h + 1, :])
        pred = jnp.sum(sd * k_col, axis=0, keepdims=True)
        delta = (v[h:h + 1, :] - pred) * gate[h:h + 1, :]
        s1 = sd + k_col * delta
        so_ref[h] = s1
        outs.append(jnp.sum(s1 * q_col, axis=0, keepdims=True))
    o = jnp.concatenate(outs, axis=0)
    d_ref[...] = _gated_norm(o, g_ref[...], z_ref[...]).astype(d_ref.dtype)


def gdn_sample(x_new, state_conv, layer, conv_w, z, gate, gnorm_g, state_ssm):
    db, ch, _ = x_new.shape
    n_heads = ch // 3
    return pl.pallas_call(
        functools.partial(_gdn_step_kernel, n_heads=n_heads),
        grid=(db,),
        in_specs=[
            pl.BlockSpec((None, ch, DK), lambda b: (b, 0, 0)),
            pl.BlockSpec((None, None, CONV - 1, ch, DK), lambda b: (layer, b, 0, 0, 0)),
            pl.BlockSpec((CONV, ch, DK), lambda b: (0, 0, 0)),
            pl.BlockSpec((None, n_heads, DV), lambda b: (b, 0, 0)),
            pl.BlockSpec((None, 2 * n_heads, 1), lambda b: (b, 0, 0)),
            pl.BlockSpec((1, DV), lambda b: (0, 0)),
            pl.BlockSpec((None, None, n_heads, DK, DV), lambda b: (layer, b, 0, 0, 0)),
        ],
        out_specs=[
            pl.BlockSpec((None, n_heads, DV), lambda b: (b, 0, 0)),
            pl.BlockSpec((None, n_heads, DK, DV), lambda b: (b, 0, 0, 0)),
        ],
        out_shape=[
            jax.ShapeDtypeStruct((db, n_heads, DV), BF16),
            jax.ShapeDtypeStruct((db, n_heads, DK, DV), F32),
        ],
        compiler_params=_params("parallel"),
        name="gdn_sample",
    )(x_new, state_conv, conv_w, z, gate, gnorm_g.reshape(1, DV), state_ssm)


def kernel(x_prompt, x_sample, cache_k, cache_v, state_conv, state_ssm, page_table, c_prompt, c_sample, w_mod, b_mod, norm_mix_g, w_in, conv_w, a_log, dt_bias, lambda_q1, lambda_k1, lambda_q2, lambda_k2, subln_g, gdn_norm_g, w_out, norm_ffn_g, w_gate, w_up, w_down, final_norm_g):
    batch, seq, d = x_prompt.shape
    db, ds, _ = x_sample.shape
    assert ds == 1
    depth = w_in.shape[0]
    n_a = cache_k.shape[3]
    n_d = state_ssm.shape[2]
    conv_ch = state_conv.shape[3]
    past = page_table.shape[1] * cache_k.shape[2]
    m_p = batch * seq
    qkv_w = n_a * HD_QK
    col_conv = 3 * qkv_w
    col_z = col_conv + conv_ch
    col_ba = col_z + n_d * DV
    hg = math.gcd(n_d, GDN_HEADS_PER_STEP)

    c_all = jnp.zeros((MOD_ROWS, d), F32).at[:batch].set(c_prompt).at[SAMPLE_ROW0:SAMPLE_ROW0 + db].set(c_sample)
    mods = adaln(c_all, w_mod, b_mod)
    mods3 = mods.reshape(depth * MOD_ROWS * N_MOD, 1, d)
    mods_s = mods.reshape(depth, MOD_ROWS, N_MOD, d)[:, SAMPLE_ROW0:SAMPLE_ROW0 + db]

    tabs_p = _rope_tables(jnp.arange(seq), seq)
    tabs_s = _rope_tables(past + jnp.arange(ds), db)

    w_in_t = jnp.swapaxes(w_in, 1, 2)
    xp = x_prompt.reshape(m_p, d)
    xs = x_sample.reshape(db, d)
    outs = [[] for _ in range(8)]
    for l in range(depth):
        lam_init = 0.8 - 0.6 * math.exp(-0.3 * l)
        lam_vecs = jnp.stack([lambda_q1[l], lambda_k1[l], lambda_q2[l], lambda_k2[l]])
        w_down_b = cast_layer(w_down, l)
        conv_w3 = conv_w[l].reshape(CONV, conv_ch // DK, DK)
        ms = mods_s[l]

        def gate_p(which):
            return lambda tm, tn: _gate_spec_prompt(tn, l, which, seq // tm)

        h = norm_mod_prompt(xp, norm_mix_g[l], mods3, l, 0, 1, seq)
        h_s = norm_mod_sample(xs, norm_mix_g[l], ms[:, 0], ms[:, 1])
        proj, proj_s = matmul_both(h, h_s, w_in_t, l, col_ba, seq)
        gcol, grow = gates(h, w_in_t, l, col_ba, a_log[l], dt_bias[l], GDN_CHUNK)
        gcol_s, _ = gates(h_s, w_in_t, l, col_ba, a_log[l], dt_bias[l], 0)

        qb, kf, kb, vf, vb = qkv_post(proj, tabs_p, n_a, seq)
        att = diff_attn_prompt(qb, kb, vb, lam_vecs, subln_g[l], lam_init, batch, seq, n_a)
        ngrp = n_d // hg
        gcol_g = jnp.concatenate([gcol[:, :n_d].reshape(m_p, ngrp, hg), gcol[:, n_d:].reshape(m_p, ngrp, hg)],
                                 axis=2).transpose(1, 0, 2)
        grow_g = jnp.concatenate([grow[:n_d].reshape(ngrp, hg, m_p), grow[n_d:].reshape(ngrp, hg, m_p)], axis=1)
        gdn_o, s_p = gdn_prompt(proj, col_conv, conv_w[l], gcol_g, grow_g, gdn_norm_g[l], batch, seq, n_d, hg=hg)
        outs[0].append(kf.reshape(batch, seq, n_a, HD_QK))
        outs[1].append(vf.reshape(batch, seq, n_a, HD_V))
        outs[2].append(proj.reshape(batch, seq, -1)[:, seq - (CONV - 1):, col_conv:col_z])
        outs[3].append(s_p)

        qb, kf, _, vf, _ = qkv_post(proj_s, tabs_s, n_a, db)
        att_s = diff_attn_sample(qb.reshape(db, n_a, HD_QK), kf.reshape(db, n_a, HD_QK), vf.reshape(db, n_a, HD_V),
                                 cache_k, cache_v, page_table, l, lam_vecs, subln_g[l], lam_init)
        cin = proj_s[:, col_conv:col_z]
        gdn_s, s_s = gdn_sample(cin.reshape(db, conv_ch // DK, DK),
                                state_conv.reshape(depth, db, CONV - 1, conv_ch // DK, DK), l, conv_w3,
                                proj_s[:, col_z:col_ba].reshape(db, n_d, DV), gcol_s.reshape(db, 2 * n_d, 1),
                                gdn_norm_g[l], state_ssm)
        outs[4].append(kf.reshape(db, ds, n_a, HD_QK))
        outs[5].append(vf.reshape(db, ds, n_a, HD_V))
        outs[6].append(jnp.concatenate([state_conv[l][:, 1:], cin[:, None, :]], axis=1))
        outs[7].append(s_s)

        xp, xs = out_proj_both(att, gdn_o, att_s.reshape(db, n_a * HD_V), gdn_s.reshape(db, n_d * DV), w_out, l,
                               xp, mods3, 2, xs, ms[:, 2], seq)
        h = norm_mod_prompt(xp, norm_ffn_g[l], mods3, l, 3, 4, seq)
        h_s = norm_mod_sample(xs, norm_ffn_g[l], ms[:, 3], ms[:, 4])
        act, act_s = gate_up_both(h, h_s, w_gate, w_up, l, seq)
        xp = matmul_residual(act, w_down_b, xp, gate_p(5), mods3, min(512, seq), 256)
        xs = matmul_residual(act_s, w_down_b, xs, lambda tm, tn: pl.BlockSpec((tm, tn), lambda i, j: (i, j)),
                             ms[:, 5], db, 256)

    y_prompt = final_norm(xp, final_norm_g).reshape(batch, seq, d)
    y_sample = final_norm(xs, final_norm_g).reshape(db, ds, d)
    return (y_prompt, y_sample) + tuple(jnp.stack(o) for o in outs)
```

```python
import functools
import math

import jax
import jax.numpy as jnp
import numpy as np
from jax import lax
from jax.experimental import pallas as pl
from jax.experimental.pallas import tpu as pltpu

F32 = jnp.float32
BF16 = jnp.bfloat16

HD_V = 128
HD_C = 64
HD_QK = 2 * HD_C
DK = 128
DV = 128
ROT_DIM = HD_C // 4
ROT_HALF = ROT_DIM // 2
ROPE_THETA = 500000.0
Q_SCALE = HD_C ** -0.5 * math.log2(math.e)
CONV = 4
GDN_CHUNK = 64
N_MOD = 6
EPS = 1e-6
MOD_ROWS = 32
SAMPLE_ROW0 = 16
GDN_HEADS_PER_STEP = 16
GDN_ROWS_PER_STEP = 128
FLASH_SLAB = 64

VMEM_LIMIT_BYTES = 56 * 1024 * 1024
LANES = 128


def _params(*sem):
    return pltpu.CompilerParams(dimension_semantics=sem, vmem_limit_bytes=VMEM_LIMIT_BYTES)


def _tile(dim, want):
    if dim <= want:
        return dim
    t = want
    while dim % t:
        t -= 8
    return t


def _sigmoid(x):
    return 0.5 * jnp.tanh(0.5 * x) + 0.5


def _silu(x):
    return x * _sigmoid(x)


def _softplus(x):
    return jnp.maximum(x, 0.0) + jnp.log(1.0 + jnp.exp(-jnp.abs(x)))


def _adaln_kernel(c_ref, w_ref, b_ref, o_ref):
    acc = jnp.dot(c_ref[...].astype(BF16), w_ref[...].astype(BF16), preferred_element_type=F32)
    o_ref[...] = acc + b_ref[...]


def adaln(c_all, w_mod, b_mod):
    depth, d, n = w_mod.shape
    tn = _tile(n, 512)
    return pl.pallas_call(
        _adaln_kernel,
        grid=(depth, n // tn),
        in_specs=[
            pl.BlockSpec((MOD_ROWS, d), lambda l, j: (0, 0)),
            pl.BlockSpec((None, d, tn), lambda l, j: (l, 0, j)),
            pl.BlockSpec((None, 1, tn), lambda l, j: (l, 0, j)),
        ],
        out_specs=pl.BlockSpec((None, MOD_ROWS, tn), lambda l, j: (l, 0, j)),
        out_shape=jax.ShapeDtypeStruct((depth, MOD_ROWS, n), F32),
        compiler_params=_params("parallel", "parallel"),
        name="adaln",
    )(c_all, w_mod, b_mod.reshape(depth, 1, n))


def _norm_mod_kernel(x_ref, g_ref, shift_ref, scale_ref, o_ref):
    x = x_ref[...]
    y = x * lax.rsqrt(jnp.mean(x * x, axis=-1, keepdims=True) + EPS) * g_ref[...]
    o_ref[...] = (y * (1.0 + scale_ref[...]) + shift_ref[...]).astype(o_ref.dtype)


def _norm_kernel(x_ref, g_ref, o_ref):
    x = x_ref[...]
    o_ref[...] = (x * lax.rsqrt(jnp.mean(x * x, axis=-1, keepdims=True) + EPS) * g_ref[...]).astype(o_ref.dtype)


def norm_mod_prompt(x, g, mods3, layer, i_shift, i_scale, seq):
    m, d = x.shape
    tm = _tile(seq, 256)
    per = seq // tm

    def mod_idx(which):
        return lambda i: ((layer * MOD_ROWS + i // per) * N_MOD + which, 0, 0)

    return pl.pallas_call(
        _norm_mod_kernel,
        grid=(m // tm,),
        in_specs=[
            pl.BlockSpec((tm, d), lambda i: (i, 0)),
            pl.BlockSpec((1, d), lambda i: (0, 0)),
            pl.BlockSpec((None, 1, d), mod_idx(i_shift)),
            pl.BlockSpec((None, 1, d), mod_idx(i_scale)),
        ],
        out_specs=pl.BlockSpec((tm, d), lambda i: (i, 0)),
        out_shape=jax.ShapeDtypeStruct((m, d), BF16),
        compiler_params=_params("parallel"),
        name="norm_mod_prompt",
    )(x, g.reshape(1, d), mods3, mods3)


def norm_mod_sample(x, g, shift, scale):
    m, d = x.shape
    full = pl.BlockSpec((m, d), lambda: (0, 0))
    return pl.pallas_call(
        _norm_mod_kernel,
        in_specs=[full, pl.BlockSpec((1, d), lambda: (0, 0)), full, full],
        out_specs=full,
        out_shape=jax.ShapeDtypeStruct((m, d), BF16),
        name="norm_mod_sample",
    )(x, g.reshape(1, d), shift, scale)


def final_norm(x, g):
    m, d = x.shape
    tm = _tile(m, 256)
    return pl.pallas_call(
        _norm_kernel,
        grid=(m // tm,),
        in_specs=[pl.BlockSpec((tm, d), lambda i: (i, 0)), pl.BlockSpec((1, d), lambda i: (0, 0))],
        out_specs=pl.BlockSpec((tm, d), lambda i: (i, 0)),
        out_shape=jax.ShapeDtypeStruct((m, d), F32),
        compiler_params=_params("parallel"),
        name="final_norm",
    )(x, g.reshape(1, d))


def _lane_tile(n, want):
    best = n
    for t in range(LANES, min(n, want) + 1, LANES):
        if n % t == 0:
            best = t
    return best


def _cast_kernel(w_ref, o_ref):
    o_ref[...] = w_ref[...].astype(o_ref.dtype)


def cast_layer(w, layer, n_cols=None):
    _, k, n = w.shape
    n_cols = n if n_cols is None else n_cols
    tk, tn = _tile(k, 256), _lane_tile(n_cols, 8192)
    return pl.pallas_call(
        _cast_kernel,
        grid=(k // tk, n_cols // tn),
        in_specs=[pl.BlockSpec((None, tk, tn), lambda i, j: (layer, i, j))],
        out_specs=pl.BlockSpec((tk, tn), lambda i, j: (i, j)),
        out_shape=jax.ShapeDtypeStruct((k, n_cols), BF16),
        compiler_params=_params("parallel", "parallel"),
        name="cast_layer",
    )(w)


def _mm_kernel(a_ref, w_ref, o_ref):
    o_ref[...] = jnp.dot(a_ref[...], w_ref[...], preferred_element_type=F32).astype(o_ref.dtype)


def matmul(a, w, n_out, tm_want=1024, tn_want=1024):
    m, k = a.shape
    tm, tn = _tile(m, tm_want), _tile(n_out, tn_want)
    return pl.pallas_call(
        _mm_kernel,
        grid=(m // tm, n_out // tn),
        in_specs=[pl.BlockSpec((tm, k), lambda i, j: (i, 0)), pl.BlockSpec((k, tn), lambda i, j: (0, j))],
        out_specs=pl.BlockSpec((tm, tn), lambda i, j: (i, j)),
        out_shape=jax.ShapeDtypeStruct((m, n_out), F32),
        compiler_params=_params("parallel", "parallel"),
        name="matmul",
    )(a, w)


def _mm_res_kernel(a_ref, w_ref, x_ref, gate_ref, o_ref):
    acc = jnp.dot(a_ref[...], w_ref[...], preferred_element_type=F32)
    o_ref[...] = x_ref[...] + gate_ref[...] * acc


def _mm2_res_kernel(a_ref, d_ref, wa_ref, wd_ref, x_ref, gate_ref, o_ref):
    acc = jnp.dot(a_ref[...], wa_ref[...], preferred_element_type=F32)
    acc = acc + jnp.dot(d_ref[...], wd_ref[...], preferred_element_type=F32)
    o_ref[...] = x_ref[...] + gate_ref[...] * acc


def _gate_spec_prompt(tn, layer, which, per):
    return pl.BlockSpec((None, 1, tn), lambda i, j: ((layer * MOD_ROWS + i // per) * N_MOD + which, 0, j))


def matmul_residual(a, w, x, gate_spec, gate, tm_want, tn_want):
    m, k = a.shape
    n = w.shape[1]
    tm, tn = _tile(m, tm_want), _tile(n, tn_want)
    return pl.pallas_call(
        _mm_res_kernel,
        grid=(m // tm, n // tn),
        in_specs=[
            pl.BlockSpec((tm, k), lambda i, j: (i, 0)),
            pl.BlockSpec((k, tn), lambda i, j: (0, j)),
            pl.BlockSpec((tm, tn), lambda i, j: (i, j)),
            gate_spec(tm, tn),
        ],
        out_specs=pl.BlockSpec((tm, tn), lambda i, j: (i, j)),
        out_shape=jax.ShapeDtypeStruct((m, n), F32),
        compiler_params=_params("parallel", "parallel"),
        name="matmul_residual",
    )(a, w, x, gate)


def matmul2_residual(a, d, w, x, gate_spec, gate, tm_want, tn_want):
    m, kh = a.shape
    n = w.shape[1]
    tm, tn = _tile(m, tm_want), _tile(n, tn_want)
    return pl.pallas_call(
        _mm2_res_kernel,
        grid=(m // tm, n // tn),
        in_specs=[
            pl.BlockSpec((tm, kh), lambda i, j: (i, 0)),
            pl.BlockSpec((tm, kh), lambda i, j: (i, 0)),
            pl.BlockSpec((kh, tn), lambda i, j: (0, j)),
            pl.BlockSpec((kh, tn), lambda i, j: (1, j)),
            pl.BlockSpec((tm, tn), lambda i, j: (i, j)),
            gate_spec(tm, tn),
        ],
        out_specs=pl.BlockSpec((tm, tn), lambda i, j: (i, j)),
        out_shape=jax.ShapeDtypeStruct((m, n), F32),
        compiler_params=_params("parallel", "parallel"),
        name="matmul2_residual",
    )(a, d, w, w, x, gate)


def _gateup_kernel(h_ref, wg_ref, wu_ref, o_ref):
    h = h_ref[...]
    g = jnp.dot(h, wg_ref[...], preferred_element_type=F32)
    u = jnp.dot(h, wu_ref[...], preferred_element_type=F32)
    o_ref[...] = (_silu(g) * u).astype(o_ref.dtype)


def gate_up(h, wg, wu, tm_want=1024, tn_want=256):
    m, k = h.shape
    n = wg.shape[1]
    tm, tn = _tile(m, tm_want), _tile(n, tn_want)
    return pl.pallas_call(
        _gateup_kernel,
        grid=(m // tm, n // tn),
        in_specs=[
            pl.BlockSpec((tm, k), lambda i, j: (i, 0)),
            pl.BlockSpec((k, tn), lambda i, j: (0, j)),
            pl.BlockSpec((k, tn), lambda i, j: (0, j)),
        ],
        out_specs=pl.BlockSpec((tm, tn), lambda i, j: (i, j)),
        out_shape=jax.ShapeDtypeStruct((m, n), BF16),
        compiler_params=_params("parallel", "parallel"),
        name="gate_up",
    )(h, wg, wu)


def _first_row_block():
    return pl.program_id(1) == 0


def _wres_mm_kernel(a_ref, as_ref, wt_ref, o_ref, os_ref, w_scr):
    @pl.when(_first_row_block())
    def _():
        w_scr[...] = wt_ref[...].T.astype(BF16)
        os_ref[...] = jnp.dot(as_ref[...], w_scr[...], preferred_element_type=F32)

    o_ref[...] = jnp.dot(a_ref[...], w_scr[...], preferred_element_type=F32)


def matmul_both(a, a_s, w_t, layer, n_out, seq, tm_want=1024, tn_want=512):
    m, k = a.shape
    ms = a_s.shape[0]
    tm, tn = _tile(m, min(tm_want, seq)), _lane_tile(n_out, tn_want)
    return pl.pallas_call(
        _wres_mm_kernel,
        grid=(n_out // tn, m // tm),
        in_specs=[
            pl.BlockSpec((tm, k), lambda j, i: (i, 0)),
            pl.BlockSpec((ms, k), lambda j, i: (0, 0)),
            pl.BlockSpec((None, tn, k), lambda j, i: (layer, j, 0)),
        ],
        out_specs=[pl.BlockSpec((tm, tn), lambda j, i: (i, j)), pl.BlockSpec((ms, tn), lambda j, i: (0, j))],
        out_shape=[jax.ShapeDtypeStruct((m, n_out), F32), jax.ShapeDtypeStruct((ms, n_out), F32)],
        scratch_shapes=[pltpu.VMEM((k, tn), BF16)],
        compiler_params=_params("arbitrary", "arbitrary"),
        name="matmul_both",
    )(a, a_s, w_t)


def _wres_gateup_kernel(h_ref, hs_ref, wg_ref, wu_ref, o_ref, os_ref, wg_scr, wu_scr):
    @pl.when(_first_row_block())
    def _():
        wg_scr[...] = wg_ref[...].astype(BF16)
        wu_scr[...] = wu_ref[...].astype(BF16)
        hs = hs_ref[...]
        gs = jnp.dot(hs, wg_scr[...], preferred_element_type=F32)
        us = jnp.dot(hs, wu_scr[...], preferred_element_type=F32)
        os_ref[...] = (_silu(gs) * us).astype(os_ref.dtype)

    h = h_ref[...]
    g = jnp.dot(h, wg_scr[...], preferred_element_type=F32)
    u = jnp.dot(h, wu_scr[...], preferred_element_type=F32)
    o_ref[...] = (_silu(g) * u).astype(o_ref.dtype)


def gate_up_both(h, h_s, wg, wu, layer, seq, tm_want=1024, tn_want=256):
    m, k = h.shape
    ms = h_s.shape[0]
    n = wg.shape[2]
    tm, tn = _tile(m, min(tm_want, seq)), _lane_tile(n, tn_want)
    wspec = pl.BlockSpec((None, k, tn), lambda j, i: (layer, 0, j))
    return pl.pallas_call(
        _wres_gateup_kernel,
        grid=(n // tn, m // tm),
        in_specs=[pl.BlockSpec((tm, k), lambda j, i: (i, 0)), pl.BlockSpec((ms, k), lambda j, i: (0, 0)), wspec, wspec],
        out_specs=[pl.BlockSpec((tm, tn), lambda j, i: (i, j)), pl.BlockSpec((ms, tn), lambda j, i: (0, j))],
        out_shape=[jax.ShapeDtypeStruct((m, n), BF16), jax.ShapeDtypeStruct((ms, n), BF16)],
        scratch_shapes=[pltpu.VMEM((k, tn), BF16)] * 2,
        compiler_params=_params("arbitrary", "arbitrary"),
        name="gate_up_both",
    )(h, h_s, wg, wu)


def _wres_out_kernel(a_ref, d_ref, as_ref, ds_ref, wa_ref, wd_ref, x_ref, gate_ref, xs_ref, gates_ref,
                     o_ref, os_ref, wa_scr, wd_scr):
    @pl.when(_first_row_block())
    def _():
        wa_scr[...] = wa_ref[...].astype(BF16)
        wd_scr[...] = wd_ref[...].astype(BF16)
        acc_s = jnp.dot(as_ref[...], wa_scr[...], preferred_element_type=F32)
        acc_s = acc_s + jnp.dot(ds_ref[...], wd_scr[...], preferred_element_type=F32)
        os_ref[...] = xs_ref[...] + gates_ref[...] * acc_s

    acc = jnp.dot(a_ref[...], wa_scr[...], preferred_element_type=F32)
    acc = acc + jnp.dot(d_ref[...], wd_scr[...], preferred_element_type=F32)
    o_ref[...] = x_ref[...] + gate_ref[...] * acc


def out_proj_both(a, d, a_s, d_s, w, layer, x, mods3, which, x_s, gate_s, seq, tm_want=1024, tn_want=512):
    m, kh = a.shape
    ms = a_s.shape[0]
    n = w.shape[2]
    tm, tn = _tile(m, min(tm_want, seq)), _lane_tile(n, tn_want)
    per = seq // tm
    half = lambda r: pl.BlockSpec((None, kh, tn), lambda j, i, r=r: (layer, r, j))
    rows = pl.BlockSpec((tm, kh), lambda j, i: (i, 0))
    rows_s = pl.BlockSpec((ms, kh), lambda j, i: (0, 0))
    tile_s = pl.BlockSpec((ms, tn), lambda j, i: (0, j))
    return pl.pallas_call(
        _wres_out_kernel,
        grid=(n // tn, m // tm),
        in_specs=[
            rows, rows, rows_s, rows_s, half(0), half(1),
            pl.BlockSpec((tm, tn), lambda j, i: (i, j)),
            pl.BlockSpec((None, 1, tn), lambda j, i: ((layer * MOD_ROWS + i // per) * N_MOD + which, 0, j)),
            tile_s, tile_s,
        ],
        out_specs=[pl.BlockSpec((tm, tn), lambda j, i: (i, j)), tile_s],
        out_shape=[jax.ShapeDtypeStruct((m, n), F32), jax.ShapeDtypeStruct((ms, n), F32)],
        scratch_shapes=[pltpu.VMEM((kh, tn), BF16)] * 2,
        compiler_params=_params("arbitrary", "arbitrary"),
        name="out_proj_both",
    )(a, d, a_s, d_s, w, w, x, mods3, x_s, gate_s)


def _gates_kernel(h_ref, w_ref, pcol_ref, prow_ref, col_ref, row_ref, *, n_heads, chunk):
    wrow = lax.broadcasted_iota(jnp.int32, w_ref.shape, 0)
    w = jnp.where(wrow < 2 * n_heads, w_ref[...], 0.0).astype(BF16)
    wide = lax.dot_general(h_ref[...], w, (((1,), (1,)), ((), ())), preferred_element_type=F32)
    col = wide[:, :2 * n_heads]
    row = wide.T[:2 * n_heads, :]

    def gate(x, is_beta, a_log, dt_bias):
        return jnp.where(is_beta, _sigmoid(x), -jnp.exp(a_log) * _softplus(x + dt_bias))

    lane = lax.broadcasted_iota(jnp.int32, col.shape, 1)
    col = gate(col, lane < n_heads, pcol_ref[0:1, :], pcol_ref[1:2, :])
    sub = lax.broadcasted_iota(jnp.int32, row.shape, 0)
    row = gate(row, sub < n_heads, prow_ref[:, 0:1], prow_ref[:, 1:2])
    if chunk:
        t_col = lax.broadcasted_iota(jnp.int32, col.shape, 0) % chunk
        t_row = lax.broadcasted_iota(jnp.int32, row.shape, 1) % chunk
        s = 1
        while s < chunk:
            col = col + jnp.where((t_col >= s) & (lane >= n_heads), pltpu.roll(col, s, axis=0), 0.0)
            row = row + jnp.where((t_row >= s) & (sub >= n_heads), pltpu.roll(row, s, axis=1), 0.0)
            s *= 2
    col_ref[...] = col
    row_ref[...] = row


def gates(h, w_t, layer, col_ba, a_log, dt_bias, chunk):
    m, k = h.shape
    n2 = w_t.shape[1] - col_ba
    nh = n2 // 2
    assert col_ba % LANES == 0 and n2 <= LANES
    tm = _tile(m, 512)
    zeros = jnp.zeros((nh,), F32)
    pcol = jnp.stack([jnp.concatenate([zeros, a_log]), jnp.concatenate([zeros, dt_bias])])
    return pl.pallas_call(
        functools.partial(_gates_kernel, n_heads=nh, chunk=chunk),
        grid=(m // tm,),
        in_specs=[
            pl.BlockSpec((tm, k), lambda i: (i, 0)),
            pl.BlockSpec((None, LANES, k), lambda i: (layer, col_ba // LANES, 0)),
            pl.BlockSpec((2, n2), lambda i: (0, 0)),
            pl.BlockSpec((n2, 2), lambda i: (0, 0)),
        ],
        out_specs=[pl.BlockSpec((tm, n2), lambda i: (i, 0)), pl.BlockSpec((n2, tm), lambda i: (0, i))],
        out_shape=[jax.ShapeDtypeStruct((m, n2), F32), jax.ShapeDtypeStruct((n2, m), F32)],
        compiler_params=_params("parallel"),
        name="gates",
    )(h, w_t, pcol, pcol.T)


def _rope_tables(pos, rows):
    inv = ROPE_THETA ** (-jnp.arange(ROT_HALF, dtype=F32) * (2.0 / ROT_DIM))
    ang = pos.astype(F32)[:, None] * inv[None, :]
    cos, sin = jnp.cos(ang), jnp.sin(ang)
    n = pos.shape[0]
    one = jnp.ones((n, HD_C - ROT_DIM), F32)
    zero = jnp.zeros((n, HD_C - ROT_DIM), F32)
    zh = jnp.zeros((n, ROT_HALF), F32)
    c = jnp.concatenate([cos, cos, one], axis=1)
    sa = jnp.concatenate([-sin, zh, zero], axis=1)
    sb = jnp.concatenate([zh, sin, zero], axis=1)
    tabs = [jnp.concatenate([t, t], axis=1) for t in (c, sa, sb)]
    if n != rows:
        tabs = [jnp.broadcast_to(t, (rows, 2 * HD_C)) for t in tabs]
    return tabs


def _qkv_post_kernel(q_ref, k_ref, v_ref, c_ref, sa_ref, sb_ref, qb_ref, kf_ref, kb_ref, vf_ref, vb_ref):
    c, sa, sb = c_ref[...], sa_ref[...], sb_ref[...]

    def rope(x):
        return x * c + pltpu.roll(x, LANES - ROT_HALF, axis=1) * sa + pltpu.roll(x, ROT_HALF, axis=1) * sb

    q = rope(q_ref[...])
    k = rope(k_ref[...])
    qb_ref[...] = (q * Q_SCALE).astype(BF16)
    kf_ref[...] = k
    kb_ref[...] = k.astype(BF16)
    v = v_ref[...]
    vf_ref[...] = v
    vb_ref[...] = v.astype(BF16)


def qkv_post(proj, tabs, n_heads, rows_per_seq):
    m = proj.shape[0]
    tm = _tile(rows_per_seq, 1024)
    per = rows_per_seq // tm
    width = n_heads * HD_QK
    blk = lambda off: pl.BlockSpec((tm, HD_QK), lambda i, h, off=off: (i, off + h))
    tab = pl.BlockSpec((tm, HD_QK), lambda i, h: (i % per, 0))
    out = pl.BlockSpec((tm, HD_QK), lambda i, h: (i, h))
    return pl.pallas_call(
        _qkv_post_kernel,
        grid=(m // tm, n_heads),
        in_specs=[blk(0), blk(n_heads), blk(2 * n_heads), tab, tab, tab],
        out_specs=[out] * 5,
        out_shape=[jax.ShapeDtypeStruct((m, width), dt) for dt in (BF16, F32, BF16, F32, BF16)],
        compiler_params=_params("parallel", "parallel"),
        name="qkv_post",
    )(proj, proj, proj, *tabs)


def _lambda_value(lam_ref, lam_init):
    l1 = jnp.sum(lam_ref[0:1, :] * lam_ref[1:2, :], axis=1, keepdims=True)
    l2 = jnp.sum(lam_ref[2:3, :] * lam_ref[3:4, :], axis=1, keepdims=True)
    return jnp.exp(l1) - jnp.exp(l2) + lam_init


def _subln(o, g, lam_init):
    y = o * lax.rsqrt(jnp.mean(o * o, axis=-1, keepdims=True) + EPS) * g
    return y * (1.0 - lam_init)


def _split_maps(q):
    lane = lax.broadcasted_iota(jnp.int32, q.shape, 1)
    zero = jnp.zeros_like(q)
    return jnp.concatenate([jnp.where(lane < HD_C, q, zero), jnp.where(lane >= HD_C, q, zero)], axis=0)


def _flash_kernel(q_ref, k_ref, v_ref, lam_ref, g_ref, o_ref, *, seq, tq, tk, hq, lam_init):
    lam = _lambda_value(lam_ref, lam_init)
    gain = g_ref[...]
    lanes = [slice(h * HD_QK, (h + 1) * HD_QK) for h in range(hq)]
    nt_dims = (((1,), (1,)), ((), ()))
    tn_dims = (((0,), (0,)), ((), ()))
    key_off = lax.broadcasted_iota(jnp.int32, (tk, 2 * tq), 0)
    qry_off = lax.broadcasted_iota(jnp.int32, (tk, 2 * tq), 1) % tq

    def scores(kb, q2):
        k0 = pl.multiple_of(kb * tk, tk)
        return tuple(lax.dot_general(k_ref[pl.ds(k0, tk), ln], qq, nt_dims, preferred_element_type=F32)
                     for ln, qq in zip(lanes, q2))

    def softmax_pv(kb, m, l, acc, s):
        k0 = pl.multiple_of(kb * tk, tk)
        m_out, l_out, acc_out = [], [], []
        for ln, mh, lh, ah, sh in zip(lanes, m, l, acc, s):
            m_new = jnp.maximum(mh, jnp.max(sh, axis=0, keepdims=True))
            alpha = jnp.exp2(mh - m_new)
            row_sum, slabs = None, []
            for r0 in range(0, tk, FLASH_SLAB):
                pj = jnp.exp2(sh[r0:r0 + FLASH_SLAB] - m_new)
                sj = jnp.sum(pj, axis=0, keepdims=True)
                row_sum = sj if row_sum is None else row_sum + sj
                slabs.append(pj.astype(BF16))
            pv = lax.dot_general(v_ref[pl.ds(k0, tk), ln], jnp.concatenate(slabs, axis=0), tn_dims,
                                 preferred_element_type=F32)
            m_out.append(m_new)
            l_out.append(alpha * lh + row_sum)
            acc_out.append(alpha * ah + pv)
        return tuple(m_out), tuple(l_out), tuple(acc_out)

    def q_block(qb, _):
        q0 = pl.multiple_of(qb * tq, tq)
        q2 = [_split_maps(q_ref[pl.ds(q0, tq), ln]) for ln in lanes]
        n_full = q0 // tk

        def full_step(kb, carry):
            m, l, acc, s = carry
            s_next = scores(kb + 1, q2)
            return softmax_pv(kb, m, l, acc, s) + (s_next,)

        init = (tuple(jnp.full((1, 2 * tq), -jnp.inf, F32) for _ in lanes),
                tuple(jnp.zeros((1, 2 * tq), F32) for _ in lanes),
                tuple(jnp.zeros((HD_V, 2 * tq), F32) for _ in lanes),
                scores(0, q2))
        m, l, acc, s = lax.fori_loop(0, n_full, full_step, init)
        keep = key_off + n_full * tk <= qry_off + q0
        s = [jnp.where(keep, a, -jnp.inf) for a in s]
        _, l, acc = softmax_pv(n_full, m, l, acc, s)
        for ln, lh, ah in zip(lanes, l, acc):
            o2 = ah / lh
            o = (o2[:, :tq] - lam * o2[:, tq:]).T
            o_ref[pl.ds(q0, tq), ln] = _subln(o, gain, lam_init).astype(o_ref.dtype)
        return 0

    lax.fori_loop(0, seq // tq, q_block, 0)


def diff_attn_prompt(qb, kb, vb, lam_vecs, subln_g, lam_init, batch, seq, n_heads, tq=256, tk=256, hq=2):
    tk = _tile(seq, tk)
    tq = _tile(tk, tq)
    hq = math.gcd(n_heads, hq)
    assert tk % tq == 0 and tq % LANES == 0
    blk = pl.BlockSpec((seq, hq * HD_QK), lambda b, g: (b, g))
    return pl.pallas_call(
        functools.partial(_flash_kernel, seq=seq, tq=tq, tk=tk, hq=hq, lam_init=lam_init),
        grid=(batch, n_heads // hq),
        in_specs=[blk, blk, blk,
                  pl.BlockSpec((4, HD_C), lambda b, g: (0, 0)),
                  pl.BlockSpec((1, HD_V), lambda b, g: (0, 0))],
        out_specs=blk,
        out_shape=jax.ShapeDtypeStruct((batch * seq, n_heads * HD_V), BF16),
        compiler_params=_params("parallel", "parallel"),
        name="diff_attn_prompt",
    )(qb, kb, vb, lam_vecs, subln_g.reshape(1, HD_V))


def _paged_kernel(pt_ref, q_ref, kn_ref, vn_ref, lam_ref, g_ref, *rest, n_heads, pages, lam_init):
    k_refs, v_refs = rest[:pages], rest[pages:2 * pages]
    o_ref, q2_scr, m_scr, l_scr, acc_scr = rest[2 * pages:]
    pg = pl.program_id(1)

    @pl.when(pg == 0)
    def _():
        q2 = _split_maps(q_ref[...])
        q2_scr[...] = q2
        kn = kn_ref[...].astype(BF16).astype(F32)
        vn = vn_ref[...].astype(BF16).astype(F32)
        s_new = jnp.sum(q2.astype(F32) * jnp.concatenate([kn, kn], axis=0), axis=1, keepdims=True)
        m_scr[...] = s_new
        l_scr[...] = jnp.ones(l_scr.shape, F32)
        acc_scr[...] = jnp.concatenate([vn, vn], axis=0)

    q2 = q2_scr[...]
    for kp_ref, vp_ref in zip(k_refs, v_refs):
        kp = kp_ref[...].reshape(-1, HD_QK).astype(BF16)
        vp = vp_ref[...].reshape(-1, HD_V).astype(BF16)
        s = lax.dot_general(q2, kp, (((1,), (1,)), ((), ())), preferred_element_type=F32)
        row_h = lax.broadcasted_iota(jnp.int32, s.shape, 0) % n_heads
        col_h = lax.broadcasted_iota(jnp.int32, s.shape, 1) % n_heads
        s = jnp.where(row_h == col_h, s, -jnp.inf)
        m_prev = m_scr[...]
        m_next = jnp.maximum(m_prev, jnp.max(s, axis=1, keepdims=True))
        alpha = jnp.exp2(m_prev - m_next)
        p = jnp.exp2(s - m_next)
        l_scr[...] = alpha * l_scr[...] + jnp.sum(p, axis=1, keepdims=True)
        acc_scr[...] = alpha * acc_scr[...] + jnp.dot(p.astype(BF16), vp, preferred_element_type=F32)
        m_scr[...] = m_next

    @pl.when(pg == pl.num_programs(1) - 1)
    def _():
        o2 = acc_scr[...] / l_scr[...]
        lam = _lambda_value(lam_ref, lam_init)
        o = o2[:n_heads] - lam * o2[n_heads:]
        o_ref[...] = _subln(o, g_ref[...], lam_init).astype(o_ref.dtype)


def diff_attn_sample(qb, k_new, v_new, cache_k, cache_v, page_table, layer, lam_vecs, subln_g, lam_init,
                     pages_per_step=8):
    db, n_heads, _ = qb.shape
    n_pages = page_table.shape[1]
    page = cache_k.shape[2]
    pps = pages_per_step
    while n_pages % pps:
        pps -= 1

    def page_spec(i):
        return pl.BlockSpec((None, None, page, n_heads, HD_QK),
                            lambda b, pg, pt, i=i: (layer, pt[b, pg * pps + i], 0, 0, 0))

    row = lambda b, pg, pt: (b, 0, 0)
    grid_spec = pltpu.PrefetchScalarGridSpec(
        num_scalar_prefetch=1,
        grid=(db, n_pages // pps),
        in_specs=[
            pl.BlockSpec((None, n_heads, HD_QK), row),
            pl.BlockSpec((None, n_heads, HD_QK), row),
            pl.BlockSpec((None, n_heads, HD_V), row),
            pl.BlockSpec((4, HD_C), lambda b, pg, pt: (0, 0)),
            pl.BlockSpec((1, HD_V), lambda b, pg, pt: (0, 0)),
        ] + [page_spec(i) for i in range(pps)] * 2,
        out_specs=pl.BlockSpec((None, n_heads, HD_V), row),
        scratch_shapes=[
            pltpu.VMEM((2 * n_heads, HD_QK), BF16),
            pltpu.VMEM((2 * n_heads, 1), F32),
            pltpu.VMEM((2 * n_heads, 1), F32),
            pltpu.VMEM((2 * n_heads, HD_V), F32),
        ],
    )
    return pl.pallas_call(
        functools.partial(_paged_kernel, n_heads=n_heads, pages=pps, lam_init=lam_init),
        grid_spec=grid_spec,
        out_shape=jax.ShapeDtypeStruct((db, n_heads, HD_V), BF16),
        compiler_params=_params("parallel", "arbitrary"),
        name="diff_attn_sample",
    )(page_table, qb, k_new, v_new, lam_vecs, subln_g.reshape(1, HD_V),
      *([cache_k] * pps), *([cache_v] * pps))


def _l2n(x):
    return x * lax.rsqrt(jnp.sum(x * x, axis=-1, keepdims=True) + EPS)


def _gated_norm(o, g, z):
    return o * lax.rsqrt(jnp.mean(o * o, axis=-1, keepdims=True) + EPS) * g * _silu(z)


def _dot_b(a, b):
    return jnp.dot(a.astype(BF16), b.astype(BF16), preferred_element_type=F32)


def _unit_lower_inverses(lows):
    c = lows[0].shape[0]
    i = lax.broadcasted_iota(jnp.int32, (c, c), 0)
    j = lax.broadcasted_iota(jnp.int32, (c, c), 1)
    eye = jnp.where(i == j, 1.0, 0.0).astype(F32)
    same16 = (i // 16) == (j // 16)
    same32 = (i // 32) == (j // 32)
    off16 = same32 & jnp.logical_not(same16)
    d = [jnp.where(same16, low, 0.0) for low in lows]
    d2 = [_dot_b(a, a) for a in d]
    d4 = [_dot_b(a, a) for a in d2]
    d8 = [_dot_b(a, a) for a in d4]
    x = [eye - a for a in d]
    x = [a + _dot_b(a, p) for a, p in zip(x, d2)]
    x = [a + _dot_b(a, p) for a, p in zip(x, d4)]
    x = [a + _dot_b(a, p) for a, p in zip(x, d8)]
    t = [_dot_b(jnp.where(off16, low, 0.0), a) for low, a in zip(lows, x)]
    x = [a - _dot_b(a, p) for a, p in zip(x, t)]
    t = [_dot_b(jnp.where(same32, 0.0, low), a) for low, a in zip(lows, x)]
    x = [a - _dot_b(a, p) for a, p in zip(x, t)]
    return x


def _causal_conv_block(x, halo, w):
    row8 = lax.broadcasted_iota(jnp.int32, halo.shape, 0)
    out = None
    for jtap in range(CONV):
        s = CONV - 1 - jtap
        if s == 0:
            xs = x
        else:
            rolled = pltpu.roll(x, s, axis=0)
            head = jnp.where(row8 < s, pltpu.roll(halo, s, axis=0), rolled[:8])
            xs = jnp.concatenate([head, rolled[8:]], axis=0)
        term = xs * w[jtap:jtap + 1, :]
        out = term if out is None else out + term
    return out


def _gdn_chunk_kernel(q_ref, k_ref, v_ref, z_ref, wq_ref, wk_ref, wv_ref, gcol_ref, grow_ref, g_ref,
                      d_ref, s_ref, hq_scr, hk_scr, hv_scr, *, hg, tb):
    ti = pl.program_id(2)
    c = GDN_CHUNK

    @pl.when(ti == 0)
    def _():
        hq_scr[...] = jnp.zeros(hq_scr.shape, F32)
        hk_scr[...] = jnp.zeros(hk_scr.shape, F32)
        hv_scr[...] = jnp.zeros(hv_scr.shape, F32)
        s_ref[...] = jnp.zeros(s_ref.shape, F32)

    xq, xk, xv = q_ref[...], k_ref[...], v_ref[...]
    aq = _silu(_causal_conv_block(xq, hq_scr[...], wq_ref[...]))
    ak = _silu(_causal_conv_block(xk, hk_scr[...], wk_ref[...]))
    av = _silu(_causal_conv_block(xv, hv_scr[...], wv_ref[...]))
    hq_scr[...] = xq[tb - 8:]
    hk_scr[...] = xk[tb - 8:]
    hv_scr[...] = xv[tb - 8:]

    ii = lax.broadcasted_iota(jnp.int32, (c, c), 0)
    jj = lax.broadcasted_iota(jnp.int32, (c, c), 1)
    gcol = gcol_ref[...]
    grow = grow_ref[...]
    gnorm = g_ref[...]

    nc = tb // c
    items = [(h, ci) for ci in range(nc) for h in range(hg)]
    lanes = lambda h: slice(h * DK, (h + 1) * DK)
    rows = lambda ci: slice(ci * c, (ci + 1) * c)
    nt_dims = (((1,), (1,)), ((), ()))
    tn_dims = (((0,), (0,)), ((), ()))

    q = [_l2n(aq[rows(ci), lanes(h)]) * (DK ** -0.5) for h, ci in items]
    k = [_l2n(ak[rows(ci), lanes(h)]) for h, ci in items]
    v = [av[rows(ci), lanes(h)] for h, ci in items]
    beta = [gcol[rows(ci), h:h + 1] for h, ci in items]
    gc_col = [gcol[rows(ci), hg + h:hg + h + 1] for h, ci in items]
    gc_row = [grow[hg + h:hg + h + 1, rows(ci)] for h, ci in items]
    decay = [jnp.exp(jnp.where(ii >= jj, a - b, -jnp.inf)) for a, b in zip(gc_col, gc_row)]
    kb = [a * b for a, b in zip(k, beta)]
    both = [lax.dot_general(jnp.concatenate([a, b], axis=0).astype(BF16), kk.astype(BF16), nt_dims,
                            preferred_element_type=F32) for a, b, kk in zip(kb, q, k)]
    low = [jnp.where(ii > jj, a[:c] * dc, 0.0) for a, dc in zip(both, decay)]
    a_in = [(a[c:] * dc).astype(BF16) for a, dc in zip(both, decay)]
    tmat = _unit_lower_inverses(low)
    eg = [jnp.exp(a) for a in gc_col]
    rhs = [jnp.concatenate([vv * b, kk * e], axis=1) for vv, b, kk, e in zip(v, beta, kb, eg)]
    uw = [_dot_b(t, r) for t, r in zip(tmat, rhs)]
    g_last = [a[c - 1:c, :] for a in gc_col]
    wq = [jnp.concatenate([a[:, DV:], qq * e], axis=0).astype(BF16) for a, qq, e in zip(uw, q, eg)]
    kt = [(kk * jnp.exp(gl - a)).astype(BF16) for kk, gl, a in zip(k, g_last, gc_col)]
    e_last = [jnp.exp(gl) for gl in g_last]

    state = [s_ref[h] for h in range(hg)]
    for ci in range(nc):
        sl = slice(ci * hg, (ci + 1) * hg)
        ws_qs = [jnp.dot(a, s.astype(BF16), preferred_element_type=F32) for a, s in zip(wq[sl], state)]
        v_new = [(a[:, :DV] - b[:c]).astype(BF16) for a, b in zip(uw[sl], ws_qs)]
        o = [b[c:] + jnp.dot(a, vn, preferred_element_type=F32) for b, a, vn in zip(ws_qs, a_in[sl], v_new)]
        state = [s * e + lax.dot_general(a, vn, tn_dims, preferred_element_type=F32)
                 for s, e, a, vn in zip(state, e_last[sl], kt[sl], v_new)]
        for h in range(hg):
            d_ref[rows(ci), lanes(h)] = _gated_norm(o[h], gnorm, z_ref[rows(ci), lanes(h)]).astype(d_ref.dtype)
    for h in range(hg):
        s_ref[h] = state[h]


def gdn_prompt(proj, col0, conv_w, gcol, grow, gnorm_g, batch, seq, n_heads, hg, tb=GDN_ROWS_PER_STEP):
    tb = _tile(seq, tb)
    nt = seq // tb
    wblk = hg * DK
    ngrp = n_heads // hg
    assert col0 % wblk == 0
    c0 = col0 // wblk

    def xin(off):
        return pl.BlockSpec((tb, wblk), lambda b, g, t, off=off: (b * nt + t, c0 + off * ngrp + g))

    def win(off):
        return pl.BlockSpec((CONV, wblk), lambda b, g, t, off=off: (0, off * ngrp + g))

    return pl.pallas_call(
        functools.partial(_gdn_chunk_kernel, hg=hg, tb=tb),
        grid=(batch, ngrp, nt),
        in_specs=[
            xin(0), xin(1), xin(2), xin(3),
            win(0), win(1), win(2),
            pl.BlockSpec((None, tb, 2 * hg), lambda b, g, t: (g, b * nt + t, 0)),
            pl.BlockSpec((None, 2 * hg, tb), lambda b, g, t: (g, 0, b * nt + t)),
            pl.BlockSpec((1, DV), lambda b, g, t: (0, 0)),
        ],
        out_specs=[
            pl.BlockSpec((tb, wblk), lambda b, g, t: (b * nt + t, g)),
            pl.BlockSpec((None, hg, DK, DV), lambda b, g, t: (b, g, 0, 0)),
        ],
        out_shape=[
            jax.ShapeDtypeStruct((batch * seq, n_heads * DV), BF16),
            jax.ShapeDtypeStruct((batch, n_heads, DK, DV), F32),
        ],
        scratch_shapes=[pltpu.VMEM((8, wblk), F32)] * 3,
        compiler_params=_params("parallel", "parallel", "arbitrary"),
        name="gdn_prompt",
    )(proj, proj, proj, proj, conv_w, conv_w, conv_w, gcol, grow, gnorm_g.reshape(1, DV))


def _gdn_step_kernel(x_ref, st_ref, w_ref, z_ref, gate_ref, g_ref, s_ref, d_ref, so_ref, *, n_heads):
    w = w_ref[...]
    conv = st_ref[0] * w[0]
    for jtap in range(1, CONV - 1):
        conv = conv + st_ref[jtap] * w[jtap]
    conv = conv + x_ref[...] * w[CONV - 1]
    act = _silu(conv)
    q = _l2n(act[:n_heads]) * (DK ** -0.5)
    k = _l2n(act[n_heads:2 * n_heads])
    v = act[2 * n_heads:]
    qt, kt = q.T, k.T
    gate = gate_ref[...]
    outs = []
    for h in range(n_heads):
        k_col, q_col = kt[:, h:h + 1], qt[:, h:h + 1]
        sd = s_ref[h] * jnp.exp(gate[n_heads + h:n_heads + h + 1, :])
        pred = jnp.sum(sd * k_col, axis=0, keepdims=True)
        delta = (v[h:h + 1, :] - pred) * gate[h:h + 1, :]
        s1 = sd + k_col * delta
        so_ref[h] = s1
        outs.append(jnp.sum(s1 * q_col, axis=0, keepdims=True))
    o = jnp.concatenate(outs, axis=0)
    d_ref[...] = _gated_norm(o, g_ref[...], z_ref[...]).astype(d_ref.dtype)


def gdn_sample(x_new, state_conv, layer, conv_w, z, gate, gnorm_g, state_ssm):
    db, ch, _ = x_new.shape
    n_heads = ch // 3
    return pl.pallas_call(
        functools.partial(_gdn_step_kernel, n_heads=n_heads),
        grid=(db,),
        in_specs=[
            pl.BlockSpec((None, ch, DK), lambda b: (b, 0, 0)),
            pl.BlockSpec((None, None, CONV - 1, ch, DK), lambda b: (layer, b, 0, 0, 0)),
            pl.BlockSpec((CONV, ch, DK), lambda b: (0, 0, 0)),
            pl.BlockSpec((None, n_heads, DV), lambda b: (b, 0, 0)),
            pl.BlockSpec((None, 2 * n_heads, 1), lambda b: (b, 0, 0)),
            pl.BlockSpec((1, DV), lambda b: (0, 0)),
            pl.BlockSpec((None, None, n_heads, DK, DV), lambda b: (layer, b, 0, 0, 0)),
        ],
        out_specs=[
            pl.BlockSpec((None, n_heads, DV), lambda b: (b, 0, 0)),
            pl.BlockSpec((None, n_heads, DK, DV), lambda b: (b, 0, 0, 0)),
        ],
        out_shape=[
            jax.ShapeDtypeStruct((db, n_heads, DV), BF16),
            jax.ShapeDtypeStruct((db, n_heads, DK, DV), F32),
        ],
        compiler_params=_params("parallel"),
        name="gdn_sample",
    )(x_new, state_conv, conv_w, z, gate, gnorm_g.reshape(1, DV), state_ssm)


def kernel(x_prompt, x_sample, cache_k, cache_v, state_conv, state_ssm, page_table, c_prompt, c_sample, w_mod, b_mod, norm_mix_g, w_in, conv_w, a_log, dt_bias, lambda_q1, lambda_k1, lambda_q2, lambda_k2, subln_g, gdn_norm_g, w_out, norm_ffn_g, w_gate, w_up, w_down, final_norm_g):
    batch, seq, d = x_prompt.shape
    db, ds, _ = x_sample.shape
    assert ds == 1
    depth = w_in.shape[0]
    n_a = cache_k.shape[3]
    n_d = state_ssm.shape[2]
    conv_ch = state_conv.shape[3]
    past = page_table.shape[1] * cache_k.shape[2]
    m_p = batch * seq
    qkv_w = n_a * HD_QK
    col_conv = 3 * qkv_w
    col_z = col_conv + conv_ch
    col_ba = col_z + n_d * DV
    hg = math.gcd(n_d, GDN_HEADS_PER_STEP)

    c_all = jnp.zeros((MOD_ROWS, d), F32).at[:batch].set(c_prompt).at[SAMPLE_ROW0:SAMPLE_ROW0 + db].set(c_sample)
    mods = adaln(c_all, w_mod, b_mod)
    mods3 = mods.reshape(depth * MOD_ROWS * N_MOD, 1, d)
    mods_s = mods.reshape(depth, MOD_ROWS, N_MOD, d)[:, SAMPLE_ROW0:SAMPLE_ROW0 + db]

    tabs_p = _rope_tables(jnp.arange(seq), seq)
    tabs_s = _rope_tables(past + jnp.arange(ds), db)

    w_in_t = jnp.swapaxes(w_in, 1, 2)
    xp = x_prompt.reshape(m_p, d)
    xs = x_sample.reshape(db, d)
    outs = [[] for _ in range(8)]
    for l in range(depth):
        lam_init = 0.8 - 0.6 * math.exp(-0.3 * l)
        lam_vecs = jnp.stack([lambda_q1[l], lambda_k1[l], lambda_q2[l], lambda_k2[l]])
        w_down_b = cast_layer(w_down, l)
        conv_w3 = conv_w[l].reshape(CONV, conv_ch // DK, DK)
        ms = mods_s[l]

        def gate_p(which):
            return lambda tm, tn: _gate_spec_prompt(tn, l, which, seq // tm)

        h = norm_mod_prompt(xp, norm_mix_g[l], mods3, l, 0, 1, seq)
        h_s = norm_mod_sample(xs, norm_mix_g[l], ms[:, 0], ms[:, 1])
        proj, proj_s = matmul_both(h, h_s, w_in_t, l, col_ba, seq)
        gcol, grow = gates(h, w_in_t, l, col_ba, a_log[l], dt_bias[l], GDN_CHUNK)
        gcol_s, _ = gates(h_s, w_in_t, l, col_ba, a_log[l], dt_bias[l], 0)

        qb, kf, kb, vf, vb = qkv_post(proj, tabs_p, n_a, seq)
        att = diff_attn_prompt(qb, kb, vb, lam_vecs, subln_g[l], lam_init, batch, seq, n_a)
        ngrp = n_d // hg
        gcol_g = jnp.concatenate([gcol[:, :n_d].reshape(m_p, ngrp, hg), gcol[:, n_d:].reshape(m_p, ngrp, hg)],
                                 axis=2).transpose(1, 0, 2)
        grow_g = jnp.concatenate([grow[:n_d].reshape(ngrp, hg, m_p), grow[n_d:].reshape(ngrp, hg, m_p)], axis=1)
        gdn_o, s_p = gdn_prompt(proj, col_conv, conv_w[l], gcol_g, grow_g, gdn_norm_g[l], batch, seq, n_d, hg=hg)
        outs[0].append(kf.reshape(batch, seq, n_a, HD_QK))
        outs[1].append(vf.reshape(batch, seq, n_a, HD_V))
        outs[2].append(proj.reshape(batch, seq, -1)[:, seq - (CONV - 1):, col_conv:col_z])
        outs[3].append(s_p)

        qb, kf, _, vf, _ = qkv_post(proj_s, tabs_s, n_a, db)
        att_s = diff_attn_sample(qb.reshape(db, n_a, HD_QK), kf.reshape(db, n_a, HD_QK), vf.reshape(db, n_a, HD_V),
                                 cache_k, cache_v, page_table, l, lam_vecs, subln_g[l], lam_init)
        cin = proj_s[:, col_conv:col_z]
        gdn_s, s_s = gdn_sample(cin.reshape(db, conv_ch // DK, DK),
                                state_conv.reshape(depth, db, CONV - 1, conv_ch // DK, DK), l, conv_w3,
                                proj_s[:, col_z:col_ba].reshape(db, n_d, DV), gcol_s.reshape(db, 2 * n_d, 1),
                                gdn_norm_g[l], state_ssm)
        outs[4].append(kf.reshape(db, ds, n_a, HD_QK))
        outs[5].append(vf.reshape(db, ds, n_a, HD_V))
        outs[6].append(jnp.concatenate([state_conv[l][:, 1:], cin[:, None, :]], axis=1))
        outs[7].append(s_s)

        xp, xs = out_proj_both(att, gdn_o, att_s.reshape(db, n_a * HD_V), gdn_s.reshape(db, n_d * DV), w_out, l,
                               xp, mods3, 2, xs, ms[:, 2], seq)
        h = norm_mod_prompt(xp, norm_ffn_g[l], mods3, l, 3, 4, seq)
        h_s = norm_mod_sample(xs, norm_ffn_g[l], ms[:, 3], ms[:, 4])
        act, act_s = gate_up_both(h, h_s, w_gate, w_up, l, seq)
        xp = matmul_residual(act, w_down_b, xp, gate_p(5), mods3, min(512, seq), 256)
        xs = matmul_residual(act_s, w_down_b, xs, lambda tm, tn: pl.BlockSpec((tm, tn), lambda i, j: (i, j)),
                             ms[:, 5], db, 256)

    y_prompt = final_norm(xp, final_norm_g).reshape(batch, seq, d)
    y_sample = final_norm(xs, final_norm_g).reshape(db, ds, d)
    return (y_prompt, y_sample) + tuple(jnp.stack(o) for o in outs)
```
